```python
import math
import jax, jax.numpy as jnp
from jax import lax
import numpy as np

D_MODEL = 4096
BATCH = 1
SEQ = 16384
DEPTH = 2

ROPE_THETA = 10000.0
NORM_EPS = 1e-6
Q_BLOCK = 128

DA_HEADS = 8
DA_DIM = 64
DA_WIDTH = DA_HEADS * 2 * DA_DIM

GDN_HEADS = 16
GDN_DK = 128
GDN_DV = 128
GDN_CONV = 4
GDN_CHUNK = 64
GDN_KEY_WIDTH = GDN_HEADS * GDN_DK
GDN_WIDTH = GDN_HEADS * GDN_DV

DSA_HEADS = 8
DSA_DIM = 128
DSA_WIDTH = DSA_HEADS * DSA_DIM
IDX_HEADS = 32
IDX_DIM = 64
TOPK_MAX = 256

MIX_WIDTH = DA_WIDTH + GDN_WIDTH + DSA_WIDTH

IN_SPLITS = (DA_WIDTH, DA_WIDTH, DA_WIDTH,
             2 * GDN_KEY_WIDTH + GDN_WIDTH, GDN_WIDTH, GDN_HEADS, GDN_HEADS,
             DSA_WIDTH, DSA_WIDTH, DSA_WIDTH, IDX_HEADS * IDX_DIM, IDX_DIM, IDX_HEADS)
IN_WIDTH = sum(IN_SPLITS)

PEER_HEADS = 8
PEER_NKEYS = 128
PEER_EXPERTS = PEER_NKEYS * PEER_NKEYS
PEER_DKEY = 256
PEER_TOPK = 16
PEER_TOKEN_BLOCK = 32

PLE_DIM = 256

kernel_name = 'hymba_style_diff_gdn_dsa_peer_block'


def rms_norm(x, g):
    xf = x.astype(jnp.float32)
    y = xf * lax.rsqrt(jnp.mean(xf * xf, axis=-1, keepdims=True) + NORM_EPS)
    return (y * g.astype(jnp.float32)).astype(x.dtype)


def l2_norm(x):
    xf = x.astype(jnp.float32)
    return (xf * lax.rsqrt(jnp.sum(xf * xf, axis=-1, keepdims=True) + NORM_EPS)).astype(x.dtype)


def rope(x, pos):
    d = x.shape[-1]
    inv = ROPE_THETA ** (-jnp.arange(0, d, 2, dtype=jnp.float32) / d)
    ang = pos[:, None] * inv[None, :]
    shape = (ang.shape[0],) + (1,) * (x.ndim - 3) + (d // 2,)
    cos = jnp.cos(ang).reshape(shape)
    sin = jnp.sin(ang).reshape(shape)
    xf = x.astype(jnp.float32)
    x1, x2 = xf[..., : d // 2], xf[..., d // 2:]
    return jnp.concatenate([x1 * cos - x2 * sin, x1 * sin + x2 * cos], axis=-1).astype(x.dtype)


def split_cols(t, sizes):
    offsets = np.cumsum(sizes)[:-1].tolist()
    return jnp.split(t, offsets, axis=-1)


def to_blocks(t):
    b, l = t.shape[:2]
    return jnp.swapaxes(t.reshape((b, l // Q_BLOCK, Q_BLOCK) + t.shape[2:]), 0, 1)


def from_blocks(t):
    nb, b = t.shape[:2]
    return jnp.swapaxes(t, 0, 1).reshape((b, nb * Q_BLOCK) + t.shape[3:])


def causal_conv(x, w):
    k, c = w.shape
    return lax.conv_general_dilated(x, w[:, None, :].astype(x.dtype), window_strides=(1,),
                                    padding=[(k - 1, 0)], dimension_numbers=('NHC', 'HIO', 'NHC'),
                                    feature_group_count=c)


def diff_attention(q, k, v, pos, q_norm_g, k_norm_g, lam_params, subln_g, lam_init):
    b, l, _ = q.shape
    q = rope(rms_norm(q.reshape(b, l, DA_HEADS, 2, DA_DIM), q_norm_g), pos)
    k = rope(rms_norm(k.reshape(b, l, DA_HEADS, 2, DA_DIM), k_norm_g), pos)
    v = v.reshape(b, l, DA_HEADS, 2 * DA_DIM)
    lp = lam_params.astype(jnp.float32)
    lam = jnp.exp(jnp.sum(lp[0] * lp[1])) - jnp.exp(jnp.sum(lp[2] * lp[3])) + lam_init
    scale = DA_DIM ** -0.5
    kpos = jnp.arange(l)

    def block(args):
        qb, start = args
        qpos = start + jnp.arange(Q_BLOCK)
        s = jnp.einsum('bqhmd,bkhmd->bhmqk', qb, k, preferred_element_type=jnp.float32) * scale
        s = jnp.where(kpos[None, :] <= qpos[:, None], s, -jnp.inf)
        pr = jax.nn.softmax(s, axis=-1)
        wts = (pr[:, :, 0] - lam * pr[:, :, 1]).astype(v.dtype)
        return jnp.einsum('bhqk,bkhe->bqhe', wts, v)

    starts = jnp.arange(l // Q_BLOCK) * Q_BLOCK
    o = from_blocks(lax.map(block, (to_blocks(q), starts)))
    o = rms_norm(o, subln_g) * (1.0 - lam_init)
    return o.reshape(b, l, DA_WIDTH)


def chunk_gated_delta_rule(q, k, v, g, beta):
    b, l, h, dk = q.shape
    dv = v.shape[-1]
    c = GDN_CHUNK
    n = l // c
    f32 = jnp.float32

    def chunks(t):
        t = jnp.moveaxis(t.astype(f32), 2, 1)
        return t.reshape((b, h, n, c) + t.shape[3:])

    q = chunks(q) * (dk ** -0.5)
    k = chunks(k)
    v = chunks(v)
    g = jnp.cumsum(chunks(g), axis=-1)
    beta = chunks(beta)
    kb = k * beta[..., None]
    vb = v * beta[..., None]
    causal = jnp.tril(jnp.ones((c, c), dtype=bool))
    strict = jnp.tril(jnp.ones((c, c), dtype=bool), -1)
    decay = jnp.exp(jnp.where(causal, g[..., :, None] - g[..., None, :], -jnp.inf))
    a_mat = jnp.where(strict, jnp.einsum('bhncd,bhnsd->bhncs', kb, k) * decay, 0.0)
    t_mat = a_mat + jnp.eye(c, dtype=f32)
    u = lax.linalg.triangular_solve(t_mat, vb, left_side=True, lower=True, unit_diagonal=True)
    w = lax.linalg.triangular_solve(t_mat, kb * jnp.exp(g)[..., None], left_side=True, lower=True,
                                    unit_diagonal=True)
    attn = jnp.einsum('bhncd,bhnsd->bhncs', q, k) * decay
    g_last = g[..., -1]
    k_dec = k * jnp.exp(g_last[..., None] - g)[..., None]
    q_dec = q * jnp.exp(g)[..., None]
    xs = tuple(jnp.moveaxis(t, 2, 0) for t in (q_dec, k_dec, u, w, attn, g_last))

    def step(state, inp):
        qd, kd, uu, ww, at, gl = inp
        v_new = uu - jnp.einsum('bhcd,bhde->bhce', ww, state)
        o = jnp.einsum('bhcd,bhde->bhce', qd, state) + jnp.einsum('bhcs,bhse->bhce', at, v_new)
        state = state * jnp.exp(gl)[..., None, None] + jnp.einsum('bhcd,bhce->bhde', kd, v_new)
        return state, o

    s0 = jnp.zeros((b, h, dk, dv), f32)
    _, o = lax.scan(step, s0, xs)
    o = jnp.moveaxis(o, 0, 2).reshape(b, h, l, dv)
    return jnp.moveaxis(o, 1, 2)


def gated_deltanet(qkv, z, b_in, a_in, conv_w, a_log, dt_bias, norm_g):
    b, l, _ = qkv.shape
    qkv = jax.nn.silu(causal_conv(qkv, conv_w))
    q, k, v = split_cols(qkv, (GDN_KEY_WIDTH, GDN_KEY_WIDTH, GDN_WIDTH))
    q = l2_norm(q.reshape(b, l, GDN_HEADS, GDN_DK))
    k = l2_norm(k.reshape(b, l, GDN_HEADS, GDN_DK))
    v = v.reshape(b, l, GDN_HEADS, GDN_DV)
    beta = jax.nn.sigmoid(b_in.astype(jnp.float32))
    g = -jnp.exp(a_log.astype(jnp.float32)) * jax.nn.softplus(a_in.astype(jnp.float32) + dt_bias.astype(jnp.float32))
    o = chunk_gated_delta_rule(q, k, v, g, beta).astype(z.dtype)
    o = rms_norm(o, norm_g) * jax.nn.silu(z.reshape(b, l, GDN_HEADS, GDN_DV))
    return o.reshape(b, l, GDN_WIDTH)


def dsa_attention(q, k, v, qi, ki, wi, pos, q_norm_g, k_norm_g):
    b, l, _ = q.shape
    q = rope(rms_norm(q.reshape(b, l, DSA_HEADS, DSA_DIM), q_norm_g), pos)
    k = rope(rms_norm(k.reshape(b, l, DSA_HEADS, DSA_DIM), k_norm_g), pos)
    v = v.reshape(b, l, DSA_HEADS, DSA_DIM)
    qi = rope(qi.reshape(b, l, IDX_HEADS, IDX_DIM), pos)
    ki = rope(ki, pos)
    wi = wi * ((IDX_HEADS * IDX_DIM) ** -0.5)
    n_sel = min(TOPK_MAX, l // 4)
    kpos = jnp.arange(l)
    scale = DSA_DIM ** -0.5

    def block(args):
        qb, qib, wib, start = args
        qpos = start + jnp.arange(Q_BLOCK)
        logits = jnp.einsum('bqhd,bkd->bqhk', qib, ki, preferred_element_type=jnp.float32)
        iscore = jnp.einsum('bqhk,bqh->bqk', jax.nn.relu(logits), wib.astype(jnp.float32))
        iscore = jnp.where((kpos[None, :] <= qpos[:, None])[None], iscore, -jnp.inf)
        vals, idx = lax.top_k(iscore, n_sel)
        valid = jnp.isfinite(vals)
        ks = jax.vmap(lambda t, ix: t[ix])(k, idx)
        vs = jax.vmap(lambda t, ix: t[ix])(v, idx)
        s = jnp.einsum('bqhd,bqjhd->bhqj', qb, ks, preferred_element_type=jnp.float32) * scale
        s = jnp.where(valid[:, None], s, -jnp.inf)
        pr = jax.nn.softmax(s, axis=-1).astype(vs.dtype)
        return jnp.einsum('bhqj,bqjhd->bqhd', pr, vs)

    starts = jnp.arange(l // Q_BLOCK) * Q_BLOCK
    o = from_blocks(lax.map(block, (to_blocks(q), to_blocks(qi), to_blocks(wi), starts)))
    return o.reshape(b, l, DSA_WIDTH)


def peer_ffn(x, w_q, sub_keys, u_tab, v_tab):
    b, l, d = x.shape
    t = b * l
    xt = x.reshape(t, d)
    q = (xt @ w_q).reshape(t, PEER_HEADS, 2, PEER_DKEY // 2)
    s = jnp.einsum('thcd,hcnd->thcn', q, sub_keys, preferred_element_type=jnp.float32)
    s1, i1 = lax.top_k(s[:, :, 0], PEER_TOPK)
    s2, i2 = lax.top_k(s[:, :, 1], PEER_TOPK)
    cand = (s1[..., :, None] + s2[..., None, :]).reshape(t, PEER_HEADS, PEER_TOPK * PEER_TOPK)
    cidx = (i1[..., :, None] * PEER_NKEYS + i2[..., None, :]).reshape(t, PEER_HEADS, PEER_TOPK * PEER_TOPK)
    top_s, top_pos = lax.top_k(cand, PEER_TOPK)
    eidx = jnp.take_along_axis(cidx, top_pos, axis=-1).reshape(t, PEER_HEADS * PEER_TOPK)
    gates = jax.nn.softmax(top_s, axis=-1).reshape(t, PEER_HEADS * PEER_TOPK).astype(x.dtype)
    nb = t // PEER_TOKEN_BLOCK

    def block(args):
        xb, eb, gb = args
        act = jax.nn.gelu(jnp.einsum('td,ted->te', xb, u_tab[eb]))
        return jnp.einsum('te,ted->td', gb * act, v_tab[eb])

    out = lax.map(block, (xt.reshape(nb, PEER_TOKEN_BLOCK, d),
                          eidx.reshape(nb, PEER_TOKEN_BLOCK, -1),
                          gates.reshape(nb, PEER_TOKEN_BLOCK, -1)))
    return out.reshape(b, l, d)


def setup_inputs(seed: int = 0) -> dict:
    key = jax.random.key(seed)
    ks = jax.random.split(key, 24)
    f32 = jnp.float32

    def nrm(k, shape, scale):
        return jax.random.normal(k, shape, f32) * scale

    def gain(k, shape):
        return 1.0 + 0.01 * jax.random.normal(k, shape, f32)

    dt = jnp.exp(jax.random.uniform(ks[10], (DEPTH, GDN_HEADS), f32, math.log(1e-3), math.log(1e-1)))
    dt_bias = dt + jnp.log(-jnp.expm1(-dt))
    return {
        'x': nrm(ks[0], (BATCH, SEQ, D_MODEL), 1.0),
        'p': nrm(ks[1], (DEPTH, BATCH, SEQ, PLE_DIM), 1.0),
        'attn_norm': gain(ks[2], (DEPTH, D_MODEL)),
        'w_in': nrm(ks[3], (DEPTH, D_MODEL, IN_WIDTH), D_MODEL ** -0.5),
        'da_q_norm': gain(ks[4], (DEPTH, DA_DIM)),
        'da_k_norm': gain(ks[5], (DEPTH, DA_DIM)),
        'da_lambda': nrm(ks[6], (DEPTH, 4, DA_DIM), 0.1),
        'da_subln': gain(ks[7], (DEPTH, 2 * DA_DIM)),
        'gdn_conv': nrm(ks[8], (DEPTH, GDN_CONV, 2 * GDN_KEY_WIDTH + GDN_WIDTH), GDN_CONV ** -0.5),
        'gdn_a_log': jnp.log(jax.random.uniform(ks[9], (DEPTH, GDN_HEADS), f32, 1.0, 16.0)),
        'gdn_dt_bias': dt_bias,
        'gdn_norm': gain(ks[11], (DEPTH, GDN_DV)),
        'dsa_q_norm': gain(ks[12], (DEPTH, DSA_DIM)),
        'dsa_k_norm': gain(ks[13], (DEPTH, DSA_DIM)),
        'w_out': nrm(ks[14], (DEPTH, MIX_WIDTH, D_MODEL), MIX_WIDTH ** -0.5),
        'ffn_norm': gain(ks[15], (DEPTH, D_MODEL)),
        'peer_w_q': nrm(ks[16], (DEPTH, D_MODEL, PEER_HEADS * PEER_DKEY), D_MODEL ** -0.5),
        'peer_sub_keys': nrm(ks[17], (DEPTH, PEER_HEADS, 2, PEER_NKEYS, PEER_DKEY // 2), (PEER_DKEY // 2) ** -0.5),
        'peer_u': nrm(ks[18], (DEPTH, PEER_EXPERTS, D_MODEL), D_MODEL ** -0.5),
        'peer_v': nrm(ks[19], (DEPTH, PEER_EXPERTS, D_MODEL), (PEER_HEADS * PEER_TOPK) ** -0.5),
        'ple_norm': gain(ks[20], (DEPTH, D_MODEL)),
        'w_ple_gate': nrm(ks[21], (DEPTH, D_MODEL, D_MODEL), D_MODEL ** -0.5),
        'w_ple_proj': nrm(ks[22], (DEPTH, PLE_DIM, D_MODEL), PLE_DIM ** -0.5),
    }


def reference(x, p, attn_norm, w_in, da_q_norm, da_k_norm, da_lambda, da_subln, gdn_conv, gdn_a_log,
              gdn_dt_bias, gdn_norm, dsa_q_norm, dsa_k_norm, w_out, ffn_norm, peer_w_q, peer_sub_keys,
              peer_u, peer_v, ple_norm, w_ple_gate, w_ple_proj):
    b, l, _ = x.shape
    pos = jnp.arange(l, dtype=jnp.float32)
    h = x
    for i in range(DEPTH):
        a = rms_norm(h, attn_norm[i])
        (da_q, da_k, da_v, g_qkv, g_z, g_b, g_a,
         c_q, c_k, c_v, c_qi, c_ki, c_w) = split_cols(a @ w_in[i], IN_SPLITS)
        lam_init = 0.8 - 0.6 * math.exp(-0.3 * i)
        o_a = diff_attention(da_q, da_k, da_v, pos, da_q_norm[i], da_k_norm[i], da_lambda[i],
                             da_subln[i], lam_init)
        o_b = gated_deltanet(g_qkv, g_z, g_b, g_a, gdn_conv[i], gdn_a_log[i], gdn_dt_bias[i], gdn_norm[i])
        o_c = dsa_attention(c_q, c_k, c_v, c_qi, c_ki, c_w, pos, dsa_q_norm[i], dsa_k_norm[i])
        h = h + jnp.concatenate([o_a, o_b, o_c], axis=-1) @ w_out[i]
        h = h + peer_ffn(rms_norm(h, ffn_norm[i]), peer_w_q[i], peer_sub_keys[i], peer_u[i], peer_v[i])
        gate = jax.nn.sigmoid(rms_norm(h, ple_norm[i]) @ w_ple_gate[i])
        h = h + gate * (p[i] @ w_ple_proj[i])
    return h
```

```python
import functools
import math

import jax
import jax.numpy as jnp
from jax import lax
from jax.experimental import pallas as pl
from jax.experimental.pallas import tpu as pltpu

F32 = jnp.float32
BF16 = jnp.bfloat16
HIGHEST = lax.Precision.HIGHEST

NORM_EPS = 1e-6
ROPE_THETA = 10000.0
LANES = 128
VMEM_LIMIT = 56 * 1024 * 1024

DA_HEADS, DA_DIM = 8, 64
DA_WIDTH = DA_HEADS * 2 * DA_DIM
GDN_HEADS, GDN_DK, GDN_DV, GDN_CONV, GDN_CHUNK = 16, 128, 128, 4, 64
GDN_WIDTH = GDN_HEADS * GDN_DV
DSA_HEADS, DSA_DIM = 8, 128
DSA_WIDTH = DSA_HEADS * DSA_DIM
IDX_HEADS, IDX_DIM = 32, 64
TOPK_MAX = 256
PEER_HEADS, PEER_NKEYS, PEER_DKEY, PEER_TOPK = 8, 128, 256, 16
PLE_DIM = 256

NEG_BIG = -1e30
INT_MIN = -(2 ** 31)

_NT = (((1,), (1,)), ((), ()))


def _cparams(*sem):
    return pltpu.CompilerParams(dimension_semantics=sem, vmem_limit_bytes=VMEM_LIMIT)


def _tile(n, pref):
    t = min(n, pref)
    assert n % t == 0, (n, pref)
    return t


def _rmsnorm_kernel(x_ref, g_ref, o_ref):
    x = x_ref[...]
    y = x * lax.rsqrt(jnp.mean(x * x, axis=-1, keepdims=True) + NORM_EPS) * g_ref[...]
    o_ref[...] = y.astype(o_ref.dtype)


def rmsnorm_bf16(x, g):
    t, d = x.shape
    tm = _tile(t, 256)
    return pl.pallas_call(
        _rmsnorm_kernel,
        out_shape=jax.ShapeDtypeStruct((t, d), BF16),
        grid=(t // tm,),
        in_specs=[pl.BlockSpec((tm, d), lambda i: (i, 0)), pl.BlockSpec((1, d), lambda i: (0, 0))],
        out_specs=pl.BlockSpec((tm, d), lambda i: (i, 0)),
        compiler_params=_cparams("parallel"),
    )(x, g.reshape(1, d))


def _matmul_kernel(a_ref, b_ref, o_ref):
    o_ref[...] = jnp.dot(a_ref[...], b_ref[...], preferred_element_type=F32)


def _matmul_res_kernel(a_ref, b_ref, r_ref, o_ref):
    o_ref[...] = r_ref[...] + jnp.dot(a_ref[...], b_ref[...], preferred_element_type=F32)


def matmul(a, b, residual=None, tm=512, tn=1024):
    m, k = a.shape
    n = b.shape[1]
    tm, tn = _tile(m, tm), _tile(n, tn)
    in_specs = [pl.BlockSpec((tm, k), lambda j, i: (i, 0)), pl.BlockSpec((k, tn), lambda j, i: (0, j))]
    args = [a, b]
    body = _matmul_kernel
    if residual is not None:
        in_specs.append(pl.BlockSpec((tm, tn), lambda j, i: (i, j)))
        args.append(residual)
        body = _matmul_res_kernel
    return pl.pallas_call(
        body,
        out_shape=jax.ShapeDtypeStruct((m, n), F32),
        grid=(n // tn, m // tm),
        in_specs=in_specs,
        out_specs=pl.BlockSpec((tm, tn), lambda j, i: (i, j)),
        compiler_params=_cparams("parallel", "parallel"),
    )(*args)


def _rope_tables(t, d):
    pos = jnp.arange(t, dtype=F32)
    inv = ROPE_THETA ** (-jnp.arange(0, d, 2, dtype=F32) / d)
    ang = pos[:, None] * inv[None, :]
    cos, sin = jnp.cos(ang), jnp.sin(ang)
    reps = LANES // d
    return (jnp.tile(jnp.concatenate([cos, cos], -1), (1, reps)),
            jnp.tile(jnp.concatenate([-sin, sin], -1), (1, reps)))


def _rope64(y, cos, sin, lane):
    partner = jnp.where((lane & 32) == 0, pltpu.roll(y, 96, 1), pltpu.roll(y, 32, 1))
    return y * cos + partner * sin


def _rope128(y, cos, sin):
    return y * cos + pltpu.roll(y, 64, 1) * sin


def _seg64_mean_sq(x, lane):
    x2 = x * x
    lo = jnp.sum(jnp.where(lane < 64, x2, 0.0), axis=-1, keepdims=True)
    hi = jnp.sum(jnp.where(lane >= 64, x2, 0.0), axis=-1, keepdims=True)
    return jnp.where(lane < 64, lo, hi) * (1.0 / 64)


def _da_prep_kernel(q_ref, k_ref, v_ref, cos_ref, sin_ref, qg_ref, kg_ref, qo_ref, ko_ref, vo_ref):
    lane = lax.broadcasted_iota(jnp.int32, (1, LANES), 1)
    cos, sin = cos_ref[...], sin_ref[...]
    for c in range(DA_WIDTH // LANES):
        sl = slice(c * LANES, (c + 1) * LANES)
        for x_ref, g_ref, o_ref, scale in ((q_ref, qg_ref, qo_ref, DA_DIM ** -0.5), (k_ref, kg_ref, ko_ref, 1.0)):
            x = x_ref[:, sl]
            y = x * lax.rsqrt(_seg64_mean_sq(x, lane) + NORM_EPS) * g_ref[...]
            y = _rope64(y, cos, sin, lane)
            o_ref[:, sl] = (y * scale).astype(BF16)
    vo_ref[...] = v_ref[...].astype(BF16)


def da_prep(proj, colblk, cos, sin, qg, kg):
    t = proj.shape[0]
    tm = _tile(t, 256)
    w = DA_WIDTH
    col = lambda c: pl.BlockSpec((tm, w), lambda i, c=c: (i, c))
    row = pl.BlockSpec((tm, LANES), lambda i: (i, 0))
    vec = pl.BlockSpec((1, LANES), lambda i: (0, 0))
    out = pl.BlockSpec((tm, w), lambda i: (i, 0))
    shp = jax.ShapeDtypeStruct((t, w), BF16)
    return pl.pallas_call(
        _da_prep_kernel,
        out_shape=(shp, shp, shp),
        grid=(t // tm,),
        in_specs=[col(colblk), col(colblk + 1), col(colblk + 2), row, row, vec, vec],
        out_specs=(out, out, out),
        compiler_params=_cparams("parallel"),
    )(proj, proj, proj, cos, sin, jnp.tile(qg, 2).reshape(1, LANES), jnp.tile(kg, 2).reshape(1, LANES))


def _softmax_step(s, v, m_ref, l_ref, acc_ref):
    m_prev = m_ref[...]
    m_new = jnp.maximum(m_prev, jnp.max(s, axis=-1, keepdims=True))
    alpha = jnp.exp(m_prev - m_new)
    p = jnp.exp(s - m_new)
    l_ref[...] = alpha * l_ref[...] + jnp.sum(p, axis=-1, keepdims=True)
    acc_ref[...] = alpha * acc_ref[...] + jnp.dot(p.astype(BF16), v, preferred_element_type=F32)
    m_ref[...] = m_new


def _da_flash_kernel(lam_init, q_ref, k_ref, v_ref, lp_ref, g_ref, o_ref, m_sc, l_sc, acc_sc):
    qi, ki = pl.program_id(1), pl.program_id(2)
    tq, tk = q_ref.shape[0], k_ref.shape[0]

    @pl.when(ki == 0)
    def _():
        m_sc[...] = jnp.full(m_sc.shape, NEG_BIG, F32)
        l_sc[...] = jnp.zeros(l_sc.shape, F32)
        acc_sc[...] = jnp.zeros(acc_sc.shape, F32)

    def step(masked):
        q = q_ref[...]
        k, v = k_ref[...], v_ref[...]
        lane = lax.broadcasted_iota(jnp.int32, (1, LANES), 1)
        if masked:
            keep = (lax.broadcasted_iota(jnp.int32, (tq, tk), 1) <= lax.broadcasted_iota(jnp.int32, (tq, tk), 0))
        for mp in range(2):
            qm = jnp.where((lane < 64) == (mp == 0), q, jnp.zeros_like(q))
            s = lax.dot_general(qm, k, _NT, preferred_element_type=F32)
            if masked:
                s = jnp.where(keep, s, NEG_BIG)
            _softmax_step(s, v, m_sc.at[mp], l_sc.at[mp], acc_sc.at[mp])

    @pl.when(ki < qi)
    def _():
        step(False)

    @pl.when(ki == qi)
    def _():
        step(True)
        lp = lp_ref[...]
        lam = (jnp.exp(jnp.sum(lp[0:1] * lp[1:2], keepdims=True))
               - jnp.exp(jnp.sum(lp[2:3] * lp[3:4], keepdims=True)) + lam_init)
        o = acc_sc[0] / l_sc[0] - lam * (acc_sc[1] / l_sc[1])
        y = o * lax.rsqrt(jnp.mean(o * o, axis=-1, keepdims=True) + NORM_EPS) * g_ref[...]
        o_ref[...] = (y * (1.0 - lam_init)).astype(o_ref.dtype)


def da_flash(q, k, v, lam_params, subln_g, lam_init):
    t = q.shape[0]
    tq = _tile(t, 512)
    nq = t // tq
    kv = pl.BlockSpec((tq, LANES), lambda h, i, j: (jnp.minimum(j, i), h))
    return pl.pallas_call(
        functools.partial(_da_flash_kernel, lam_init),
        out_shape=jax.ShapeDtypeStruct((t, DA_WIDTH), BF16),
        grid=(DA_HEADS, nq, nq),
        in_specs=[pl.BlockSpec((tq, LANES), lambda h, i, j: (i, h)), kv, kv,
                  pl.BlockSpec((4, DA_DIM), lambda h, i, j: (0, 0)),
                  pl.BlockSpec((1, LANES), lambda h, i, j: (0, 0))],
        out_specs=pl.BlockSpec((tq, LANES), lambda h, i, j: (i, h)),
        scratch_shapes=[pltpu.VMEM((2, tq, 1), F32), pltpu.VMEM((2, tq, 1), F32), pltpu.VMEM((2, tq, LANES), F32)],
        compiler_params=_cparams("parallel", "parallel", "arbitrary"),
    )(q, k, v, lam_params, subln_g.reshape(1, LANES))


GDN_GROUP = 256


def _softplus(x):
    return jnp.maximum(x, 0.0) + jnp.log1p(jnp.exp(-jnp.abs(x)))


def _gdn_gates_kernel(b_ref, a_ref, at_ref, alr_ref, dtr_ref, alc_ref, dtc_ref,
                      beta_ref, gcum_ref, grev_ref, gcumt_ref, egt_ref):
    n = b_ref.shape[0]
    r = lax.broadcasted_iota(jnp.int32, (n, n), 0)
    c = lax.broadcasted_iota(jnp.int32, (n, n), 1)
    same = (r // GDN_CHUNK) == (c // GDN_CHUNK)
    incl = jnp.where(same & (c <= r), 1.0, 0.0)
    rev = jnp.where(same & (c > r), 1.0, 0.0)
    beta_ref[...] = jax.nn.sigmoid(b_ref[...])
    g = -jnp.exp(alr_ref[...]) * _softplus(a_ref[...] + dtr_ref[...])
    gcum_ref[...] = jnp.dot(incl, g, precision=HIGHEST, preferred_element_type=F32)
    grev_ref[...] = jnp.dot(rev, g, precision=HIGHEST, preferred_element_type=F32)
    gt = -jnp.exp(alc_ref[...]) * _softplus(at_ref[...] + dtc_ref[...])
    gcumt = lax.dot_general(gt, incl, _NT, precision=HIGHEST, preferred_element_type=F32)
    gcumt_ref[...] = gcumt
    egt_ref[...] = jnp.exp(gcumt)


def gdn_gates(b_in, a_in, a_t, a_log, dt_bias):
    t, h = b_in.shape
    n = _tile(t, GDN_GROUP)
    colspec = pl.BlockSpec((n, h), lambda i: (i, 0))
    rowspec = pl.BlockSpec((h, n), lambda i: (0, i))
    vr = pl.BlockSpec((1, h), lambda i: (0, 0))
    vc = pl.BlockSpec((h, 1), lambda i: (0, 0))
    cs = jax.ShapeDtypeStruct((t, h), F32)
    rs = jax.ShapeDtypeStruct((h, t), F32)
    return pl.pallas_call(
        _gdn_gates_kernel,
        out_shape=(cs, cs, cs, rs, rs),
        grid=(t // n,),
        in_specs=[colspec, colspec, rowspec, vr, vr, vc, vc],
        out_specs=(colspec, colspec, colspec, rowspec, rowspec),
        compiler_params=_cparams("parallel"),
    )(b_in, a_in, a_t, a_log.reshape(1, h), dt_bias.reshape(1, h), a_log.reshape(h, 1), dt_bias.reshape(h, 1))


def _gdn_local_kernel(xq_ref, xk_ref, xv_ref, hq_ref, hk_ref, hv_ref, wq_ref, wk_ref, wv_ref,
                      beta_ref, gcum_ref, grev_ref, gcumt_ref,
                      u_ref, w_ref, qd_ref, kdt_ref, attn_ref):
    i, h = pl.program_id(0), pl.program_id(1)
    n = xq_ref.shape[0]

    def conv_silu(x_ref, halo_ref, cw_ref):
        x, cw = x_ref[...], cw_ref[...]
        halo = jnp.where(i > 0, halo_ref[...], 0.0)

        def taps(z):
            return (cw[3:4] * z + cw[2:3] * pltpu.roll(z, 1, 0) + cw[1:2] * pltpu.roll(z, 2, 0)
                    + cw[0:1] * pltpu.roll(z, 3, 0))

        top = taps(jnp.concatenate([halo, x[:8]], axis=0))[8:16]
        y = jnp.concatenate([top, taps(x)[8:]], axis=0)
        return y * jax.nn.sigmoid(y)

    def l2n(y):
        return y * lax.rsqrt(jnp.sum(y * y, axis=-1, keepdims=True) + NORM_EPS)

    q = l2n(conv_silu(xq_ref, hq_ref, wq_ref)) * (GDN_DK ** -0.5)
    k = l2n(conv_silu(xk_ref, hk_ref, wk_ref))
    v = conv_silu(xv_ref, hv_ref, wv_ref)

    head = lax.broadcasted_iota(jnp.int32, (1, GDN_HEADS), 1) == h
    pick = lambda ref: jnp.sum(jnp.where(head, ref[...], 0.0), axis=-1, keepdims=True)
    beta, gc, grev = pick(beta_ref), pick(gcum_ref), pick(grev_ref)
    gr = gcumt_ref[pl.ds(h, 1), :]

    r = lax.broadcasted_iota(jnp.int32, (n, n), 0)
    c = lax.broadcasted_iota(jnp.int32, (n, n), 1)
    same = (r // GDN_CHUNK) == (c // GDN_CHUNK)
    decay = jnp.exp(jnp.where(same & (c <= r), gc - gr, NEG_BIG))
    kb = k * beta
    k16 = k.astype(BF16)
    kk = lax.dot_general(kb.astype(BF16), k16, _NT, preferred_element_type=F32)
    qk = lax.dot_general(q.astype(BF16), k16, _NT, preferred_element_type=F32)
    a_mat = jnp.where(same & (c < r), kk * decay, 0.0)
    attn = qk * decay

    pw = -a_mat
    tinv = jnp.where(r == c, 1.0, 0.0) + pw
    for _ in range(5):
        p16 = pw.astype(BF16)
        pw = jnp.dot(p16, p16, preferred_element_type=F32)
        tinv = tinv + jnp.dot(tinv.astype(BF16), pw.astype(BF16), preferred_element_type=F32)
    t16 = tinv.astype(BF16)
    u_ref[...] = jnp.dot(t16, (v * beta).astype(BF16), preferred_element_type=F32)
    w_ref[...] = jnp.dot(t16, (kb * jnp.exp(gc)).astype(BF16), preferred_element_type=F32).astype(BF16)
    qd_ref[...] = (q * jnp.exp(gc)).astype(BF16)
    kdt_ref[...] = (k * jnp.exp(grev)).T.astype(BF16)
    left = attn[:, :LANES]
    for j in range(1, n // LANES):
        left = left + attn[:, j * LANES:(j + 1) * LANES]
    attn_ref[...] = (left + pltpu.roll(left, 64, 1)).astype(BF16)


def gdn_local(proj, qblk, conv_w, beta, gcum, grev, gcumt):
    t = proj.shape[0]
    n = _tile(t, GDN_GROUP)
    nh = GDN_HEADS
    x = lambda part: pl.BlockSpec((n, LANES), lambda i, h, part=part: (i, qblk + part * nh + h))
    halo = lambda part: pl.BlockSpec(
        (8, LANES), lambda i, h, part=part: (jnp.maximum(i * (n // 8) - 1, 0), qblk + part * nh + h))
    cw = lambda part: pl.BlockSpec((GDN_CONV, LANES), lambda i, h, part=part: (0, part * nh + h))
    col = pl.BlockSpec((n, nh), lambda i, h: (i, 0))
    row = pl.BlockSpec((nh, n), lambda i, h: (0, i))
    out = pl.BlockSpec((n, LANES), lambda i, h: (i, h))
    f32o = jax.ShapeDtypeStruct((t, GDN_WIDTH), F32)
    b16o = jax.ShapeDtypeStruct((t, GDN_WIDTH), BF16)
    return pl.pallas_call(
        _gdn_local_kernel,
        out_shape=(f32o, b16o, b16o, jax.ShapeDtypeStruct((GDN_WIDTH, t), BF16), b16o),
        grid=(t // n, nh),
        in_specs=[x(0), x(1), x(2), halo(0), halo(1), halo(2), cw(0), cw(1), cw(2), col, col, col, row],
        out_specs=(out, out, out, pl.BlockSpec((LANES, n), lambda i, h: (h, i)), out),
        compiler_params=_cparams("parallel", "parallel"),
    )(proj, proj, proj, proj, proj, proj, conv_w, conv_w, conv_w, beta, gcum, grev, gcumt)


def _gdn_scan_kernel(egl_ref, u_ref, w_ref, qd_ref, kdt_ref, attn_ref, z_ref, g_ref, o_ref, s_ref):
    h, i = pl.program_id(0), pl.program_id(1)
    n = u_ref.shape[0]
    cpg = n // GDN_CHUNK

    @pl.when(i == 0)
    def _():
        s_ref[...] = jnp.zeros(s_ref.shape, F32)

    state = s_ref[...]
    zeros = jnp.zeros((GDN_CHUNK, GDN_DV), BF16)
    for c in range(cpg):
        sl = slice(c * GDN_CHUNK, (c + 1) * GDN_CHUNK)
        s16 = state.astype(BF16)
        vnew = u_ref[sl, :] - jnp.dot(w_ref[sl, :], s16, preferred_element_type=F32)
        v16 = vnew.astype(BF16)
        o = (jnp.dot(qd_ref[sl, :], s16, preferred_element_type=F32)
             + jnp.dot(attn_ref[sl, :GDN_CHUNK], v16, preferred_element_type=F32))
        pair = slice((c // 2) * LANES, (c // 2 + 1) * LANES)
        vpad = jnp.concatenate([v16, zeros] if c % 2 == 0 else [zeros, v16], axis=0)
        state = state * egl_ref[h, i * cpg + c] + jnp.dot(kdt_ref[:, pair], vpad, preferred_element_type=F32)
        y = o * lax.rsqrt(jnp.mean(o * o, axis=-1, keepdims=True) + NORM_EPS) * g_ref[...]
        z = z_ref[sl, :]
        o_ref[sl, :] = (y * (z * jax.nn.sigmoid(z))).astype(o_ref.dtype)
    s_ref[...] = state


def gdn_scan(egl, u, w, qd, kdt, attn, proj, zblk, norm_g):
    t = u.shape[0]
    n = _tile(t, GDN_GROUP)
    blk = pl.BlockSpec((n, LANES), lambda h, i: (i, h))
    return pl.pallas_call(
        _gdn_scan_kernel,
        out_shape=jax.ShapeDtypeStruct((t, GDN_WIDTH), BF16),
        grid=(GDN_HEADS, t // n),
        in_specs=[pl.BlockSpec(memory_space=pltpu.SMEM), blk, blk, blk,
                  pl.BlockSpec((LANES, n), lambda h, i: (h, i)), blk,
                  pl.BlockSpec((n, LANES), lambda h, i: (i, zblk + h)),
                  pl.BlockSpec((1, LANES), lambda h, i: (0, 0))],
        out_specs=blk,
        scratch_shapes=[pltpu.VMEM((GDN_DK, GDN_DV), F32)],
        compiler_params=_cparams("parallel", "arbitrary"),
    )(egl, u, w, qd, kdt, attn, proj, norm_g.reshape(1, LANES))


def gated_deltanet(proj, small_b, small_a, qblk, zblk, conv_w, a_log, dt_bias, norm_g):
    beta, gcum, grev, gcumt, egt = gdn_gates(small_b, small_a, small_a.T, a_log, dt_bias)
    u, w, qd, kdt, attn = gdn_local(proj, qblk, conv_w, beta, gcum, grev, gcumt)
    egl = egt[:, GDN_CHUNK - 1::GDN_CHUNK]
    return gdn_scan(egl, u, w, qd, kdt, attn, proj, zblk, norm_g)


DSA_SEL_TQ = 128
DSA_SEL_TK = 256


def _dsa_prep_kernel(q_ref, k_ref, v_ref, qi_ref, ki_ref, wi_ref, cos64_ref, sin64_ref, cos128_ref, sin128_ref,
                     qg_ref, kg_ref, qo_ref, ko_ref, vo_ref, qio_ref, kio_ref, wio_ref):
    lane = lax.broadcasted_iota(jnp.int32, (1, LANES), 1)
    cos64, sin64 = cos64_ref[...], sin64_ref[...]
    cos128, sin128 = cos128_ref[...], sin128_ref[...]
    for c in range(DSA_HEADS):
        sl = slice(c * LANES, (c + 1) * LANES)
        for x_ref, g_ref, o_ref in ((q_ref, qg_ref, qo_ref), (k_ref, kg_ref, ko_ref)):
            x = x_ref[:, sl]
            y = x * lax.rsqrt(jnp.mean(x * x, axis=-1, keepdims=True) + NORM_EPS) * g_ref[...]
            o_ref[:, sl] = _rope128(y, cos128, sin128).astype(BF16)
    vo_ref[...] = v_ref[...].astype(BF16)
    for c in range(IDX_HEADS * IDX_DIM // LANES):
        y = _rope64(qi_ref[:, c * LANES:(c + 1) * LANES], cos64, sin64, lane).astype(BF16)
        qio_ref[2 * c] = y[:, :IDX_DIM]
        qio_ref[2 * c + 1] = y[:, IDX_DIM:]
    kio_ref[...] = _rope64(ki_ref[...], cos64, sin64, lane).astype(BF16)
    wio_ref[...] = wi_ref[...] * ((IDX_HEADS * IDX_DIM) ** -0.5)


def dsa_prep(proj, qblk, qiblk, ki, wi, tabs64, tabs128, qg, kg):
    t = proj.shape[0]
    tm = _tile(t, 256)
    w = DSA_WIDTH
    col = lambda c: pl.BlockSpec((tm, w), lambda i, c=c: (i, c))
    row = pl.BlockSpec((tm, LANES), lambda i: (i, 0))
    vec = pl.BlockSpec((1, LANES), lambda i: (0, 0))
    out = pl.BlockSpec((tm, w), lambda i: (i, 0))
    shp = jax.ShapeDtypeStruct((t, w), BF16)
    return pl.pallas_call(
        _dsa_prep_kernel,
        out_shape=(shp, shp, shp, jax.ShapeDtypeStruct((IDX_HEADS, t, IDX_DIM), BF16),
                   jax.ShapeDtypeStruct((t, LANES), BF16), jax.ShapeDtypeStruct((t, IDX_HEADS), F32)),
        grid=(t // tm,),
        in_specs=[col(qblk), col(qblk + 1), col(qblk + 2),
                  pl.BlockSpec((tm, IDX_HEADS * IDX_DIM), lambda i: (i, qiblk)), row,
                  pl.BlockSpec((tm, IDX_HEADS), lambda i: (i, 0)), row, row, row, row, vec, vec],
        out_specs=(out, out, out, pl.BlockSpec((IDX_HEADS, tm, IDX_DIM), lambda i: (0, i, 0)), row,
                   pl.BlockSpec((tm, IDX_HEADS), lambda i: (i, 0))),
        compiler_params=_cparams("parallel"),
    )(proj, proj, proj, proj, ki, wi, *tabs64, *tabs128, qg.reshape(1, LANES), kg.reshape(1, LANES))


def _dsa_select_kernel(n_sel, qi_ref, ki_ref, wi_ref, mask_ref, keys_sc):
    i = pl.program_id(0)
    tq, tk = DSA_SEL_TQ, DSA_SEL_TK
    t = ki_ref.shape[0]
    nvalid = ((i + 1) * tq + tk - 1) // tk
    qpos = i * tq + lax.broadcasted_iota(jnp.int32, (tq, 1), 0)
    w = wi_ref[...]

    def score_chunk(c, carry):
        kc = ki_ref[pl.ds(pl.multiple_of(c * tk, tk), tk), :IDX_DIM]
        acc = jnp.zeros((tq, tk), F32)
        for h in range(IDX_HEADS):
            lg = lax.dot_general(qi_ref[h], kc, _NT, preferred_element_type=F32)
            acc = acc + w[:, h:h + 1] * jnp.maximum(lg, 0.0)
        bits = pltpu.bitcast(acc, jnp.int32)
        key = jnp.where(bits < 0, bits ^ 0x7FFFFFFF, bits)
        kpos = c * tk + lax.broadcasted_iota(jnp.int32, (1, tk), 1)
        keys_sc[:, pl.ds(pl.multiple_of(c * tk, tk), tk)] = jnp.where(kpos <= qpos, key, INT_MIN)
        return carry

    lax.fori_loop(0, nvalid, score_chunk, 0)

    def count_ge(thr):
        thr_b = jnp.broadcast_to(thr, (tq, LANES))

        def body(c, acc):
            blk = keys_sc[:, pl.ds(pl.multiple_of(c * tk, tk), tk)]
            for j in range(tk // LANES):
                acc = acc + jnp.where(blk[:, j * LANES:(j + 1) * LANES] >= thr_b, 1, 0)
            return acc

        acc = lax.fori_loop(0, nvalid, body, jnp.zeros((tq, LANES), jnp.int32))
        return jnp.sum(acc, axis=-1, keepdims=True)

    lo0 = jnp.full((tq, 1), INT_MIN + 1, jnp.int32)
    hi0 = jnp.full((tq, 1), 2 ** 31 - 1, jnp.int32)
    cnt0 = count_ge(lo0)

    def unsettled(lo, hi, cnt_lo):
        return jnp.max(jnp.where((cnt_lo > n_sel) & (lo + 1 != hi), 1, 0))

    def bisect(carry):
        lo, hi, cnt_lo, _ = carry
        mid = (lo >> 1) + (hi >> 1) + (lo & hi & 1)
        cnt = count_ge(mid)
        ok = cnt >= n_sel
        lo, hi, cnt_lo = jnp.where(ok, mid, lo), jnp.where(ok, hi, mid), jnp.where(ok, cnt, cnt_lo)
        return lo, hi, cnt_lo, unsettled(lo, hi, cnt_lo)

    lo, _, _, _ = lax.while_loop(lambda cr: cr[3] > 0, bisect, (lo0, hi0, cnt0, unsettled(lo0, hi0, cnt0)))

    thr_b = jnp.broadcast_to(lo, (tq, tk))

    def write_valid(c, carry):
        sl = pl.ds(pl.multiple_of(c * tk, tk), tk)
        mask_ref[:, sl] = jnp.where(keys_sc[:, sl] >= thr_b, 1.0, 0.0).astype(mask_ref.dtype)
        return carry

    def write_zero(c, carry):
        mask_ref[:, pl.ds(pl.multiple_of(c * tk, tk), tk)] = jnp.zeros((tq, tk), mask_ref.dtype)
        return carry

    lax.fori_loop(0, nvalid, write_valid, 0)
    lax.fori_loop(nvalid, t // tk, write_zero, 0)


def dsa_select(qi, ki, wi, n_sel):
    t = ki.shape[0]
    tq = DSA_SEL_TQ
    return pl.pallas_call(
        functools.partial(_dsa_select_kernel, n_sel),
        out_shape=jax.ShapeDtypeStruct((t, t), BF16),
        grid=(t // tq,),
        in_specs=[pl.BlockSpec((IDX_HEADS, tq, IDX_DIM), lambda i: (0, i, 0)),
                  pl.BlockSpec((t, LANES), lambda i: (0, 0)),
                  pl.BlockSpec((tq, IDX_HEADS), lambda i: (i, 0))],
        out_specs=pl.BlockSpec((tq, t), lambda i: (i, 0)),
        scratch_shapes=[pltpu.VMEM((tq, t), jnp.int32)],
        compiler_params=_cparams("parallel"),
    )(qi, ki, wi)


def _dsa_flash_kernel(q_ref, k_ref, v_ref, mask_ref, o_ref, m_sc, l_sc, acc_sc):
    qi, ki = pl.program_id(1), pl.program_id(2)

    @pl.when(ki == 0)
    def _():
        m_sc[...] = jnp.full(m_sc.shape, NEG_BIG, F32)
        l_sc[...] = jnp.zeros(l_sc.shape, F32)
        acc_sc[...] = jnp.zeros(acc_sc.shape, F32)

    @pl.when(ki <= qi)
    def _():
        s = lax.dot_general(q_ref[...], k_ref[...], _NT, preferred_element_type=F32) * (DSA_DIM ** -0.5)
        s = jnp.where(mask_ref[...] > 0, s, NEG_BIG)
        _softmax_step(s, v_ref[...], m_sc, l_sc, acc_sc)

    @pl.when(ki == qi)
    def _():
        o_ref[...] = (acc_sc[...] / l_sc[...]).astype(o_ref.dtype)


def dsa_flash(q, k, v, mask):
    t = q.shape[0]
    tq = _tile(t, 512)
    nq = t // tq
    kv = pl.BlockSpec((tq, LANES), lambda h, i, j: (jnp.minimum(j, i), h))
    return pl.pallas_call(
        _dsa_flash_kernel,
        out_shape=jax.ShapeDtypeStruct((t, DSA_WIDTH), BF16),
        grid=(DSA_HEADS, nq, nq),
        in_specs=[pl.BlockSpec((tq, LANES), lambda h, i, j: (i, h)), kv, kv,
                  pl.BlockSpec((tq, tq), lambda h, i, j: (i, jnp.minimum(j, i)))],
        out_specs=pl.BlockSpec((tq, LANES), lambda h, i, j: (i, h)),
        scratch_shapes=[pltpu.VMEM((tq, 1), F32), pltpu.VMEM((tq, 1), F32), pltpu.VMEM((tq, LANES), F32)],
        compiler_params=_cparams("parallel", "parallel", "arbitrary"),
    )(q, k, v, mask)


def dsa_attention(proj, qblk, qiblk, ki, wi, tabs64, tabs128, qg, kg):
    t = proj.shape[0]
    q, k, v, qi, kir, wis = dsa_prep(proj, qblk, qiblk, ki, wi, tabs64, tabs128, qg, kg)
    mask = dsa_select(qi, kir, wis, min(TOPK_MAX, t // 4))
    return dsa_flash(q, k, v, mask)


def _take_top(s, count):
    n = s.shape[0]
    idx = lax.broadcasted_iota(jnp.int32, s.shape, 0)
    vals = []
    for _ in range(count):
        m = jnp.max(s, axis=0, keepdims=True)
        first = jnp.min(jnp.where(s == m, idx, n), axis=0, keepdims=True)
        s = jnp.where(idx == first, NEG_BIG, s)
        vals.append(m)
    return vals


def _peer_route_kernel(q_ref, sk_ref, s1_ref, s2_ref, e1_ref, e2_ref, tau_ref):
    for h in range(PEER_HEADS):
        halves = []
        for c in range(2):
            sl = slice((2 * h + c) * LANES, (2 * h + c + 1) * LANES)
            halves.append(lax.dot_general(sk_ref[h, c], q_ref[:, sl].astype(BF16), _NT, preferred_element_type=F32))
        s1, s2 = halves
        v1, v2 = _take_top(s1, PEER_TOPK), _take_top(s2, PEER_TOPK)
        v2_all = jnp.concatenate(v2, axis=0)
        cand = jnp.concatenate([v1[a] + v2_all for a in range(PEER_TOPK)], axis=0)
        top = _take_top(cand, PEER_TOPK)
        z = sum(jnp.exp(tv - top[0]) for tv in top)
        s1_ref[h], s2_ref[h] = s1, s2
        e1_ref[h] = jnp.exp(s1 - v1[0])
        e2_ref[h] = jnp.exp(s2 - v2[0]) / z
        tau_ref[h:h + 1, :] = top[-1]


def peer_route(q, sub_keys):
    t = q.shape[0]
    tm = _tile(t, 256)
    big = pl.BlockSpec((PEER_HEADS, PEER_NKEYS, tm), lambda i: (0, 0, i))
    bs = jax.ShapeDtypeStruct((PEER_HEADS, PEER_NKEYS, t), F32)
    return pl.pallas_call(
        _peer_route_kernel,
        out_shape=(bs, bs, bs, bs, jax.ShapeDtypeStruct((PEER_HEADS, t), F32)),
        grid=(t // tm,),
        in_specs=[pl.BlockSpec((tm, PEER_HEADS * PEER_DKEY), lambda i: (i, 0)),
                  pl.BlockSpec(sub_keys.shape, lambda i: (0, 0, 0, 0))],
        out_specs=(big, big, big, big, pl.BlockSpec((PEER_HEADS, tm), lambda i: (0, i))),
        compiler_params=_cparams("parallel"),
    )(q, sub_keys)


def _gelu_tanh(x):
    return 0.5 * x * (1.0 + jnp.tanh(math.sqrt(2.0 / math.pi) * (x + 0.044715 * (x * x * x))))


def _peer_expert_kernel(x_ref, u_ref, vt_ref, s1_ref, s2_ref, e1_ref, e2_ref, tau_ref, o_ref):
    j = pl.program_id(1)
    te = u_ref.shape[0]

    @pl.when(j == 0)
    def _():
        o_ref[...] = jnp.zeros(o_ref.shape, F32)

    act = _gelu_tanh(lax.dot_general(u_ref[...], x_ref[...], _NT, preferred_element_type=F32))
    rows_per = te // PEER_NKEYS
    gates = []
    for a in range(rows_per):
        n1 = j * rows_per + a
        g = None
        for h in range(PEER_HEADS):
            pair = s1_ref[h, pl.ds(n1, 1), :] + s2_ref[h]
            gh = jnp.where(pair >= tau_ref[h:h + 1, :], e1_ref[h, pl.ds(n1, 1), :] * e2_ref[h], 0.0)
            g = gh if g is None else g + gh
        gates.append(g)
    ga = (jnp.concatenate(gates, axis=0) * act).astype(BF16)
    o_ref[...] += jnp.dot(vt_ref[...], ga, preferred_element_type=F32)


def peer_expert(x, u, vt, s1, s2, e1, e2, tau, tm=512, te=512):
    t, d = x.shape
    e = u.shape[0]
    tm, te = _tile(t, tm), _tile(e, te)
    once = pl.Buffered(1)
    big = pl.BlockSpec((PEER_HEADS, PEER_NKEYS, tm), lambda i, j: (0, 0, i), pipeline_mode=once)
    return pl.pallas_call(
        _peer_expert_kernel,
        out_shape=jax.ShapeDtypeStruct((d, t), F32),
        grid=(t // tm, e // te),
        in_specs=[pl.BlockSpec((tm, d), lambda i, j: (i, 0), pipeline_mode=once),
                  pl.BlockSpec((te, d), lambda i, j: (j, 0)),
                  pl.BlockSpec((d, te), lambda i, j: (0, j)),
                  big, big, big, big,
                  pl.BlockSpec((PEER_HEADS, tm), lambda i, j: (0, i), pipeline_mode=once)],
        out_specs=pl.BlockSpec((d, tm), lambda i, j: (0, i)),
        compiler_params=_cparams("parallel", "arbitrary"),
    )(x, u, vt, s1, s2, e1, e2, tau)


def _add_t_rmsnorm_kernel(h_ref, pt_ref, g_ref, ho_ref, ao_ref):
    x = h_ref[...] + pt_ref[...].T
    ho_ref[...] = x
    ao_ref[...] = (x * lax.rsqrt(jnp.mean(x * x, axis=-1, keepdims=True) + NORM_EPS) * g_ref[...]).astype(BF16)


def add_t_rmsnorm(h, pt, g):
    t, d = h.shape
    tm = _tile(t, 256)
    blk = pl.BlockSpec((tm, d), lambda i: (i, 0))
    return pl.pallas_call(
        _add_t_rmsnorm_kernel,
        out_shape=(jax.ShapeDtypeStruct((t, d), F32), jax.ShapeDtypeStruct((t, d), BF16)),
        grid=(t // tm,),
        in_specs=[blk, pl.BlockSpec((d, tm), lambda i: (0, i)), pl.BlockSpec((1, d), lambda i: (0, 0))],
        out_specs=(blk, blk),
        compiler_params=_cparams("parallel"),
    )(h, pt, g.reshape(1, d))


def _ple_kernel(a_ref, wg_ref, p_ref, wp_ref, h_ref, o_ref):
    gate = jax.nn.sigmoid(jnp.dot(a_ref[...], wg_ref[...], preferred_element_type=F32))
    o_ref[...] = h_ref[...] + gate * jnp.dot(p_ref[...], wp_ref[...], preferred_element_type=F32)


def ple(a, wg, p, wp, h, tm=512, tn=1024):
    m, k = a.shape
    n = wg.shape[1]
    kp = p.shape[1]
    tm, tn = _tile(m, tm), _tile(n, tn)
    return pl.pallas_call(
        _ple_kernel,
        out_shape=jax.ShapeDtypeStruct((m, n), F32),
        grid=(n // tn, m // tm),
        in_specs=[pl.BlockSpec((tm, k), lambda j, i: (i, 0)), pl.BlockSpec((k, tn), lambda j, i: (0, j)),
                  pl.BlockSpec((tm, kp), lambda j, i: (i, 0)), pl.BlockSpec((kp, tn), lambda j, i: (0, j)),
                  pl.BlockSpec((tm, tn), lambda j, i: (i, j))],
        out_specs=pl.BlockSpec((tm, tn), lambda j, i: (i, j)),
        compiler_params=_cparams("parallel", "parallel"),
    )(a, wg, p, wp, h)


_QBLK_GDN, _ZBLK_GDN = 0, 48
_BLK_DA = 8
_BLK_DSA = 11
_BLK_IDX = 7


def _regroup_w_in(w):
    sizes = (DA_WIDTH, DA_WIDTH, DA_WIDTH, 3 * GDN_WIDTH, GDN_WIDTH, GDN_HEADS, GDN_HEADS,
             DSA_WIDTH, DSA_WIDTH, DSA_WIDTH, IDX_HEADS * IDX_DIM, IDX_DIM, IDX_HEADS)
    offs = [0]
    for s in sizes:
        offs.append(offs[-1] + s)
    (da_q, da_k, da_v, g_qkv, g_z, g_b, g_a, c_q, c_k, c_v, c_qi, c_ki, c_w) = [
        w[:, offs[n]:offs[n + 1]] for n in range(len(sizes))]
    main = jnp.concatenate([g_qkv, g_z, da_q, da_k, da_v, c_q, c_k, c_v, c_qi], axis=1).astype(BF16)
    small = jnp.concatenate([c_ki, c_w, g_b, g_a], axis=1).astype(BF16)
    return main, small


def kernel(x, p, attn_norm, w_in, da_q_norm, da_k_norm, da_lambda, da_subln, gdn_conv, gdn_a_log, gdn_dt_bias, gdn_norm, dsa_q_norm, dsa_k_norm, w_out, ffn_norm, peer_w_q, peer_sub_keys, peer_u, peer_v, ple_norm, w_ple_gate, w_ple_proj):
    b, t, d = x.shape
    assert b == 1
    depth = w_in.shape[0]
    tabs64, tabs128 = _rope_tables(t, 64), _rope_tables(t, 128)
    h = x.reshape(t, d)
    for i in range(depth):
        w_main, w_small = _regroup_w_in(w_in[i])
        a = rmsnorm_bf16(h, attn_norm[i])
        proj = matmul(a, w_main)
        small = matmul(a, w_small)
        lam_init = 0.8 - 0.6 * math.exp(-0.3 * i)
        qa, ka, va = da_prep(proj, _BLK_DA, *tabs64, da_q_norm[i], da_k_norm[i])
        o_a = da_flash(qa, ka, va, da_lambda[i], da_subln[i], lam_init)
        o_b = gated_deltanet(proj, small[:, 96:112], small[:, 112:128], _QBLK_GDN, _ZBLK_GDN, gdn_conv[i],
                             gdn_a_log[i], gdn_dt_bias[i], gdn_norm[i])
        o_c = dsa_attention(proj, _BLK_DSA, _BLK_IDX, small, small[:, 64:96], tabs64, tabs128,
                            dsa_q_norm[i], dsa_k_norm[i])
        mix = jnp.concatenate([o_a, o_b, o_c], axis=-1)
        h = matmul(mix, w_out[i].astype(BF16), residual=h)

        a = rmsnorm_bf16(h, ffn_norm[i])
        pq = matmul(a, peer_w_q[i].astype(BF16))
        s1, s2, e1, e2, tau = peer_route(pq, peer_sub_keys[i].astype(BF16))
        pt = peer_expert(a, peer_u[i].astype(BF16), peer_v[i].T.astype(BF16), s1, s2, e1, e2, tau)
        h, a = add_t_rmsnorm(h, pt, ple_norm[i])
        h = ple(a, w_ple_gate[i].astype(BF16), p[i, 0].astype(BF16), w_ple_proj[i].astype(BF16), h)
    return h.reshape(b, t, d)
```

```python
import functools
import math

import jax
import jax.numpy as jnp
from jax import lax
from jax.experimental import pallas as pl
from jax.experimental.pallas import tpu as pltpu

F32 = jnp.float32
BF16 = jnp.bfloat16
HIGHEST = lax.Precision.HIGHEST

NORM_EPS = 1e-6
ROPE_THETA = 10000.0
LANES = 128
VMEM_LIMIT = 56 * 1024 * 1024

DA_HEADS, DA_DIM = 8, 64
DA_WIDTH = DA_HEADS * 2 * DA_DIM
GDN_HEADS, GDN_DK, GDN_DV, GDN_CONV, GDN_CHUNK = 16, 128, 128, 4, 64
GDN_WIDTH = GDN_HEADS * GDN_DV
DSA_HEADS, DSA_DIM = 8, 128
DSA_WIDTH = DSA_HEADS * DSA_DIM
IDX_HEADS, IDX_DIM = 32, 64
TOPK_MAX = 256
PEER_HEADS, PEER_NKEYS, PEER_DKEY, PEER_TOPK = 8, 128, 256, 16
PLE_DIM = 256

NEG_BIG = -1e30
INT_MIN = -(2 ** 31)

_NT = (((1,), (1,)), ((), ()))


def _cparams(*sem):
    return pltpu.CompilerParams(dimension_semantics=sem, vmem_limit_bytes=VMEM_LIMIT)


def _tile(n, pref):
    t = min(n, pref)
    assert n % t == 0, (n, pref)
    return t


def _rmsnorm_kernel(x_ref, g_ref, o_ref):
    x = x_ref[...]
    y = x * lax.rsqrt(jnp.mean(x * x, axis=-1, keepdims=True) + NORM_EPS) * g_ref[...]
    o_ref[...] = y.astype(o_ref.dtype)


def rmsnorm_bf16(x, g):
    t, d = x.shape
    tm = _tile(t, 256)
    return pl.pallas_call(
        _rmsnorm_kernel,
        out_shape=jax.ShapeDtypeStruct((t, d), BF16),
        grid=(t // tm,),
        in_specs=[pl.BlockSpec((tm, d), lambda i: (i, 0)), pl.BlockSpec((1, d), lambda i: (0, 0))],
        out_specs=pl.BlockSpec((tm, d), lambda i: (i, 0)),
        compiler_params=_cparams("parallel"),
    )(x, g.reshape(1, d))


def _matmul_kernel(a_ref, b_ref, o_ref):
    o_ref[...] = jnp.dot(a_ref[...], b_ref[...], preferred_element_type=F32)


def _matmul_res_kernel(a_ref, b_ref, r_ref, o_ref):
    o_ref[...] = r_ref[...] + jnp.dot(a_ref[...], b_ref[...], preferred_element_type=F32)


def matmul(a, b, residual=None, tm=512, tn=1024):
    m, k = a.shape
    n = b.shape[1]
    tm, tn = _tile(m, tm), _tile(n, tn)
    in_specs = [pl.BlockSpec((tm, k), lambda j, i: (i, 0)), pl.BlockSpec((k, tn), lambda j, i: (0, j))]
    args = [a, b]
    body = _matmul_kernel
    if residual is not None:
        in_specs.append(pl.BlockSpec((tm, tn), lambda j, i: (i, j)))
        args.append(residual)
        body = _matmul_res_kernel
    return pl.pallas_call(
        body,
        out_shape=jax.ShapeDtypeStruct((m, n), F32),
        grid=(n // tn, m // tm),
        in_specs=in_specs,
        out_specs=pl.BlockSpec((tm, tn), lambda j, i: (i, j)),
        compiler_params=_cparams("parallel", "parallel"),
    )(*args)


def _rope_tables(t, d):
    pos = jnp.arange(t, dtype=F32)
    inv = ROPE_THETA ** (-jnp.arange(0, d, 2, dtype=F32) / d)
    ang = pos[:, None] * inv[None, :]
    cos, sin = jnp.cos(ang), jnp.sin(ang)
    reps = LANES // d
    return (jnp.tile(jnp.concatenate([cos, cos], -1), (1, reps)),
            jnp.tile(jnp.concatenate([-sin, sin], -1), (1, reps)))


def _rope64(y, cos, sin, lane):
    partner = jnp.where((lane & 32) == 0, pltpu.roll(y, 96, 1), pltpu.roll(y, 32, 1))
    return y * cos + partner * sin


def _rope128(y, cos, sin):
    return y * cos + pltpu.roll(y, 64, 1) * sin


def _seg64_mean_sq(x, lane):
    x2 = x * x
    lo = jnp.sum(jnp.where(lane < 64, x2, 0.0), axis=-1, keepdims=True)
    hi = jnp.sum(jnp.where(lane >= 64, x2, 0.0), axis=-1, keepdims=True)
    return jnp.where(lane < 64, lo, hi) * (1.0 / 64)


def _da_prep_kernel(q_ref, k_ref, v_ref, cos_ref, sin_ref, qg_ref, kg_ref, qo_ref, ko_ref, vo_ref):
    lane = lax.broadcasted_iota(jnp.int32, (1, LANES), 1)
    cos, sin = cos_ref[...], sin_ref[...]
    for c in range(DA_WIDTH // LANES):
        sl = slice(c * LANES, (c + 1) * LANES)
        for x_ref, g_ref, o_ref, scale in ((q_ref, qg_ref, qo_ref, DA_DIM ** -0.5), (k_ref, kg_ref, ko_ref, 1.0)):
            x = x_ref[:, sl]
            y = x * lax.rsqrt(_seg64_mean_sq(x, lane) + NORM_EPS) * g_ref[...]
            y = _rope64(y, cos, sin, lane)
            o_ref[:, sl] = (y * scale).astype(BF16)
    vo_ref[...] = v_ref[...].T.astype(BF16)


def da_prep(proj, colblk, cos, sin, qg, kg):
    t = proj.shape[0]
    tm = _tile(t, 256)
    w = DA_WIDTH
    col = lambda c: pl.BlockSpec((tm, w), lambda i, c=c: (i, c))
    row = pl.BlockSpec((tm, LANES), lambda i: (i, 0))
    vec = pl.BlockSpec((1, LANES), lambda i: (0, 0))
    out = pl.BlockSpec((tm, w), lambda i: (i, 0))
    shp = jax.ShapeDtypeStruct((t, w), BF16)
    return pl.pallas_call(
        _da_prep_kernel,
        out_shape=(shp, shp, jax.ShapeDtypeStruct((w, t), BF16)),
        grid=(t // tm,),
        in_specs=[col(colblk), col(colblk + 1), col(colblk + 2), row, row, vec, vec],
        out_specs=(out, out, pl.BlockSpec((w, tm), lambda i: (0, i))),
        compiler_params=_cparams("parallel"),
    )(proj, proj, proj, cos, sin, jnp.tile(qg, 2).reshape(1, LANES), jnp.tile(kg, 2).reshape(1, LANES))


ATTN_TILE = 1024


def _causal_steps(t, tile):
    n = t // tile
    pairs = [(i, j) for i in range(n) for j in range(i + 1)]
    return (jnp.asarray([a for a, _ in pairs], jnp.int32), jnp.asarray([b for _, b in pairs], jnp.int32))


def _softmax_step_t(st, vt, idx, m_sc, l_sc, acc_sc):
    m_prev = m_sc[idx]
    m_new = jnp.maximum(m_prev, jnp.max(st, axis=0, keepdims=True))
    alpha = jnp.exp(m_prev - m_new)
    p = jnp.exp(st - m_new)
    l_sc[idx] = alpha * l_sc[idx] + jnp.sum(p, axis=0, keepdims=True)
    acc_sc[idx] = alpha * acc_sc[idx] + jnp.dot(vt, p.astype(BF16), preferred_element_type=F32)
    m_sc[idx] = m_new


def _init_softmax_stats(m_sc, l_sc, acc_sc):
    m_sc[...] = jnp.full(m_sc.shape, NEG_BIG, F32)
    l_sc[...] = jnp.zeros(l_sc.shape, F32)
    acc_sc[...] = jnp.zeros(acc_sc.shape, F32)


def _da_flash_kernel(lam_init, qi_of, kj_of, q_ref, k_ref, vt_ref, lp_ref, g_ref, o_ref, m_sc, l_sc, acc_sc):
    step = pl.program_id(0)
    qi, kj = qi_of[step], kj_of[step]
    tq, tk = q_ref.shape[0], k_ref.shape[0]

    @pl.when(kj == 0)
    def _():
        _init_softmax_stats(m_sc, l_sc, acc_sc)

    lane = lax.broadcasted_iota(jnp.int32, (1, LANES), 1)

    def all_heads(masked):
        def one_head(h, carry):
            c = pl.multiple_of(h * LANES, LANES)
            q, k, vt = q_ref[:, pl.ds(c, LANES)], k_ref[:, pl.ds(c, LANES)], vt_ref[pl.ds(c, LANES), :]
            for mp in range(2):
                qm = jnp.where((lane < 64) == (mp == 0), q, jnp.zeros_like(q))
                st = lax.dot_general(k, qm, _NT, preferred_element_type=F32)
                if masked:
                    keep = (lax.broadcasted_iota(jnp.int32, (tk, tq), 0) <= lax.broadcasted_iota(jnp.int32, (tk, tq), 1))
                    st = jnp.where(keep, st, NEG_BIG)
                _softmax_step_t(st, vt, 2 * h + mp, m_sc, l_sc, acc_sc)
            return carry

        lax.fori_loop(0, DA_HEADS, one_head, 0)

    @pl.when(kj < qi)
    def _():
        all_heads(False)

    @pl.when(kj == qi)
    def _():
        all_heads(True)
        lp = lp_ref[...]
        lam = (jnp.exp(jnp.sum(lp[0:1] * lp[1:2], keepdims=True))
               - jnp.exp(jnp.sum(lp[2:3] * lp[3:4], keepdims=True)) + lam_init)

        def finish(h, carry):
            o = acc_sc[2 * h] / l_sc[2 * h] - lam * (acc_sc[2 * h + 1] / l_sc[2 * h + 1])
            y = o * lax.rsqrt(jnp.mean(o * o, axis=0, keepdims=True) + NORM_EPS) * g_ref[...]
            o_ref[:, pl.ds(pl.multiple_of(h * LANES, LANES), LANES)] = (y * (1.0 - lam_init)).T.astype(o_ref.dtype)
            return carry

        lax.fori_loop(0, DA_HEADS, finish, 0)


def da_flash(q, k, vt, lam_params, subln_g, lam_init):
    t = q.shape[0]
    tile = _tile(t, ATTN_TILE)
    qi_of, kj_of = _causal_steps(t, tile)
    grid_spec = pltpu.PrefetchScalarGridSpec(
        num_scalar_prefetch=2,
        grid=(qi_of.shape[0],),
        in_specs=[pl.BlockSpec((tile, DA_WIDTH), lambda s, qi, kj: (qi[s], 0)),
                  pl.BlockSpec((tile, DA_WIDTH), lambda s, qi, kj: (kj[s], 0)),
                  pl.BlockSpec((DA_WIDTH, tile), lambda s, qi, kj: (0, kj[s])),
                  pl.BlockSpec((4, DA_DIM), lambda s, qi, kj: (0, 0)),
                  pl.BlockSpec((LANES, 1), lambda s, qi, kj: (0, 0))],
        out_specs=pl.BlockSpec((tile, DA_WIDTH), lambda s, qi, kj: (qi[s], 0)),
        scratch_shapes=[pltpu.VMEM((2 * DA_HEADS, 1, tile), F32), pltpu.VMEM((2 * DA_HEADS, 1, tile), F32),
                        pltpu.VMEM((2 * DA_HEADS, LANES, tile), F32)])
    return pl.pallas_call(
        functools.partial(_da_flash_kernel, lam_init),
        out_shape=jax.ShapeDtypeStruct((t, DA_WIDTH), BF16),
        grid_spec=grid_spec,
        compiler_params=_cparams("arbitrary"),
    )(qi_of, kj_of, q, k, vt, lam_params, subln_g.reshape(LANES, 1))


GDN_GROUP = 256


def _softplus(x):
    return jnp.maximum(x, 0.0) + jnp.log1p(jnp.exp(-jnp.abs(x)))


def _gdn_gates_kernel(b_ref, a_ref, at_ref, alr_ref, dtr_ref, alc_ref, dtc_ref,
                      beta_ref, gcum_ref, grev_ref, gcumt_ref, egt_ref):
    n = b_ref.shape[0]
    r = lax.broadcasted_iota(jnp.int32, (n, n), 0)
    c = lax.broadcasted_iota(jnp.int32, (n, n), 1)
    same = (r // GDN_CHUNK) == (c // GDN_CHUNK)
    incl = jnp.where(same & (c <= r), 1.0, 0.0)
    rev = jnp.where(same & (c > r), 1.0, 0.0)
    beta_ref[...] = jax.nn.sigmoid(b_ref[...])
    g = -jnp.exp(alr_ref[...]) * _softplus(a_ref[...] + dtr_ref[...])
    gcum_ref[...] = jnp.dot(incl, g, precision=HIGHEST, preferred_element_type=F32)
    grev_ref[...] = jnp.dot(rev, g, precision=HIGHEST, preferred_element_type=F32)
    gt = -jnp.exp(alc_ref[...]) * _softplus(at_ref[...] + dtc_ref[...])
    gcumt = lax.dot_general(gt, incl, _NT, precision=HIGHEST, preferred_element_type=F32)
    gcumt_ref[...] = gcumt
    egt_ref[...] = jnp.exp(gcumt)


def gdn_gates(b_in, a_in, a_t, a_log, dt_bias):
    t, h = b_in.shape
    n = _tile(t, GDN_GROUP)
    colspec = pl.BlockSpec((n, h), lambda i: (i, 0))
    rowspec = pl.BlockSpec((h, n), lambda i: (0, i))
    vr = pl.BlockSpec((1, h), lambda i: (0, 0))
    vc = pl.BlockSpec((h, 1), lambda i: (0, 0))
    cs = jax.ShapeDtypeStruct((t, h), F32)
    rs = jax.ShapeDtypeStruct((h, t), F32)
    return pl.pallas_call(
        _gdn_gates_kernel,
        out_shape=(cs, cs, cs, rs, rs),
        grid=(t // n,),
        in_specs=[colspec, colspec, rowspec, vr, vr, vc, vc],
        out_specs=(colspec, colspec, colspec, rowspec, rowspec),
        compiler_params=_cparams("parallel"),
    )(b_in, a_in, a_t, a_log.reshape(1, h), dt_bias.reshape(1, h), a_log.reshape(h, 1), dt_bias.reshape(h, 1))


def _gdn_local_kernel(xq_ref, xk_ref, xv_ref, hq_ref, hk_ref, hv_ref, wq_ref, wk_ref, wv_ref,
                      beta_ref, gcum_ref, grev_ref, gcumt_ref,
                      u_ref, w_ref, qd_ref, kdt_ref, attn_ref):
    i, h = pl.program_id(0), pl.program_id(1)
    n = xq_ref.shape[0]

    def conv_silu(x_ref, halo_ref, cw_ref):
        x, cw = x_ref[...], cw_ref[...]
        halo = jnp.where(i > 0, halo_ref[...], 0.0)

        def taps(z):
            return (cw[3:4] * z + cw[2:3] * pltpu.roll(z, 1, 0) + cw[1:2] * pltpu.roll(z, 2, 0)
                    + cw[0:1] * pltpu.roll(z, 3, 0))

        top = taps(jnp.concatenate([halo, x[:8]], axis=0))[8:16]
        y = jnp.concatenate([top, taps(x)[8:]], axis=0)
        return y * jax.nn.sigmoid(y)

    def l2n(y):
        return y * lax.rsqrt(jnp.sum(y * y, axis=-1, keepdims=True) + NORM_EPS)

    q = l2n(conv_silu(xq_ref, hq_ref, wq_ref)) * (GDN_DK ** -0.5)
    k = l2n(conv_silu(xk_ref, hk_ref, wk_ref))
    v = conv_silu(xv_ref, hv_ref, wv_ref)

    head = lax.broadcasted_iota(jnp.int32, (1, GDN_HEADS), 1) == h
    pick = lambda ref: jnp.sum(jnp.where(head, ref[...], 0.0), axis=-1, keepdims=True)
    beta, gc, grev = pick(beta_ref), pick(gcum_ref), pick(grev_ref)
    gr = gcumt_ref[pl.ds(h, 1), :]

    r = lax.broadcasted_iota(jnp.int32, (n, n), 0)
    c = lax.broadcasted_iota(jnp.int32, (n, n), 1)
    same = (r // GDN_CHUNK) == (c // GDN_CHUNK)
    decay = jnp.exp(jnp.where(same & (c <= r), gc - gr, NEG_BIG))
    kb = k * beta
    k16 = k.astype(BF16)
    kk = lax.dot_general(kb.astype(BF16), k16, _NT, preferred_element_type=F32)
    qk = lax.dot_general(q.astype(BF16), k16, _NT, preferred_element_type=F32)
    a_mat = jnp.where(same & (c < r), kk * decay, 0.0)
    attn = qk * decay

    pw = -a_mat
    tinv = jnp.where(r == c, 1.0, 0.0) + pw
    for _ in range(5):
        p16 = pw.astype(BF16)
        pw = jnp.dot(p16, p16, preferred_element_type=F32)
        tinv = tinv + jnp.dot(tinv.astype(BF16), pw.astype(BF16), preferred_element_type=F32)
    t16 = tinv.astype(BF16)
    u_ref[...] = jnp.dot(t16, (v * beta).astype(BF16), preferred_element_type=F32)
    w_ref[...] = jnp.dot(t16, (kb * jnp.exp(gc)).astype(BF16), preferred_element_type=F32).astype(BF16)
    qd_ref[...] = (q * jnp.exp(gc)).astype(BF16)
    kdt_ref[...] = (k * jnp.exp(grev)).T.astype(BF16)
    left = attn[:, :LANES]
    for j in range(1, n // LANES):
        left = left + attn[:, j * LANES:(j + 1) * LANES]
    attn_ref[...] = (left + pltpu.roll(left, 64, 1)).astype(BF16)


def gdn_local(proj, qblk, conv_w, beta, gcum, grev, gcumt):
    t = proj.shape[0]
    n = _tile(t, GDN_GROUP)
    nh = GDN_HEADS
    x = lambda part: pl.BlockSpec((n, LANES), lambda i, h, part=part: (i, qblk + part * nh + h))
    halo = lambda part: pl.BlockSpec(
        (8, LANES), lambda i, h, part=part: (jnp.maximum(i * (n // 8) - 1, 0), qblk + part * nh + h))
    cw = lambda part: pl.BlockSpec((GDN_CONV, LANES), lambda i, h, part=part: (0, part * nh + h))
    col = pl.BlockSpec((n, nh), lambda i, h: (i, 0))
    row = pl.BlockSpec((nh, n), lambda i, h: (0, i))
    out = pl.BlockSpec((n, LANES), lambda i, h: (i, h))
    f32o = jax.ShapeDtypeStruct((t, GDN_WIDTH), F32)
    b16o = jax.ShapeDtypeStruct((t, GDN_WIDTH), BF16)
    return pl.pallas_call(
        _gdn_local_kernel,
        out_shape=(f32o, b16o, b16o, jax.ShapeDtypeStruct((GDN_WIDTH, t), BF16), b16o),
        grid=(t // n, nh),
        in_specs=[x(0), x(1), x(2), halo(0), halo(1), halo(2), cw(0), cw(1), cw(2), col, col, col, row],
        out_specs=(out, out, out, pl.BlockSpec((LANES, n), lambda i, h: (h, i)), out),
        compiler_params=_cparams("parallel", "parallel"),
    )(proj, proj, proj, proj, proj, proj, conv_w, conv_w, conv_w, beta, gcum, grev, gcumt)


def _gdn_scan_kernel(egl_ref, u_ref, w_ref, qd_ref, kdt_ref, attn_ref, z_ref, g_ref, o_ref, s_ref):
    h, i = pl.program_id(0), pl.program_id(1)
    n = u_ref.shape[0]
    cpg = n // GDN_CHUNK

    @pl.when(i == 0)
    def _():
        s_ref[...] = jnp.zeros(s_ref.shape, F32)

    state = s_ref[...]
    zeros = jnp.zeros((GDN_CHUNK, GDN_DV), BF16)
    for c in range(cpg):
        sl = slice(c * GDN_CHUNK, (c + 1) * GDN_CHUNK)
        s16 = state.astype(BF16)
        vnew = u_ref[sl, :] - jnp.dot(w_ref[sl, :], s16, preferred_element_type=F32)
        v16 = vnew.astype(BF16)
        o = (jnp.dot(qd_ref[sl, :], s16, preferred_element_type=F32)
             + jnp.dot(attn_ref[sl, :GDN_CHUNK], v16, preferred_element_type=F32))
        pair = slice((c // 2) * LANES, (c // 2 + 1) * LANES)
        vpad = jnp.concatenate([v16, zeros] if c % 2 == 0 else [zeros, v16], axis=0)
        state = state * egl_ref[h, i * cpg + c] + jnp.dot(kdt_ref[:, pair], vpad, preferred_element_type=F32)
        y = o * lax.rsqrt(jnp.mean(o * o, axis=-1, keepdims=True) + NORM_EPS) * g_ref[...]
        z = z_ref[sl, :]
        o_ref[sl, :] = (y * (z * jax.nn.sigmoid(z))).astype(o_ref.dtype)
    s_ref[...] = state


def gdn_scan(egl, u, w, qd, kdt, attn, proj, zblk, norm_g):
    t = u.shape[0]
    n = _tile(t, GDN_GROUP)
    blk = pl.BlockSpec((n, LANES), lambda h, i: (i, h))
    return pl.pallas_call(
        _gdn_scan_kernel,
        out_shape=jax.ShapeDtypeStruct((t, GDN_WIDTH), BF16),
        grid=(GDN_HEADS, t // n),
        in_specs=[pl.BlockSpec(memory_space=pltpu.SMEM), blk, blk, blk,
                  pl.BlockSpec((LANES, n), lambda h, i: (h, i)), blk,
                  pl.BlockSpec((n, LANES), lambda h, i: (i, zblk + h)),
                  pl.BlockSpec((1, LANES), lambda h, i: (0, 0))],
        out_specs=blk,
        scratch_shapes=[pltpu.VMEM((GDN_DK, GDN_DV), F32)],
        compiler_params=_cparams("parallel", "arbitrary"),
    )(egl, u, w, qd, kdt, attn, proj, norm_g.reshape(1, LANES))


def gated_deltanet(proj, small_b, small_a, qblk, zblk, conv_w, a_log, dt_bias, norm_g):
    beta, gcum, grev, gcumt, egt = gdn_gates(small_b, small_a, small_a.T, a_log, dt_bias)
    u, w, qd, kdt, attn = gdn_local(proj, qblk, conv_w, beta, gcum, grev, gcumt)
    egl = egt[:, GDN_CHUNK - 1::GDN_CHUNK]
    return gdn_scan(egl, u, w, qd, kdt, attn, proj, zblk, norm_g)


DSA_SEL_TQ = 128
DSA_SEL_TK = 256


def _dsa_prep_kernel(q_ref, k_ref, v_ref, qi_ref, ki_ref, wi_ref, cos64_ref, sin64_ref, cos128_ref, sin128_ref,
                     qg_ref, kg_ref, qo_ref, ko_ref, vo_ref, qio_ref, kio_ref, wio_ref):
    lane = lax.broadcasted_iota(jnp.int32, (1, LANES), 1)
    cos64, sin64 = cos64_ref[...], sin64_ref[...]
    cos128, sin128 = cos128_ref[...], sin128_ref[...]
    for c in range(DSA_HEADS):
        sl = slice(c * LANES, (c + 1) * LANES)
        for x_ref, g_ref, o_ref in ((q_ref, qg_ref, qo_ref), (k_ref, kg_ref, ko_ref)):
            x = x_ref[:, sl]
            y = x * lax.rsqrt(jnp.mean(x * x, axis=-1, keepdims=True) + NORM_EPS) * g_ref[...]
            o_ref[:, sl] = _rope128(y, cos128, sin128).astype(BF16)
    vo_ref[...] = v_ref[...].T.astype(BF16)
    tq = DSA_SEL_TQ
    for c in range(IDX_HEADS * IDX_DIM // LANES):
        y = _rope64(qi_ref[:, c * LANES:(c + 1) * LANES], cos64, sin64, lane).astype(BF16)
        for b in range(qi_ref.shape[0] // tq):
            for half in range(2):
                h = 2 * c + half
                qio_ref[b, h * tq:(h + 1) * tq, :] = y[b * tq:(b + 1) * tq, half * IDX_DIM:(half + 1) * IDX_DIM]
    kio_ref[...] = _rope64(ki_ref[...], cos64, sin64, lane).astype(BF16)
    wio_ref[...] = wi_ref[...] * ((IDX_HEADS * IDX_DIM) ** -0.5)


def dsa_prep(proj, qblk, qiblk, ki, wi, tabs64, tabs128, qg, kg):
    t = proj.shape[0]
    tm = _tile(t, 256)
    w = DSA_WIDTH
    nb = tm // DSA_SEL_TQ
    col = lambda c: pl.BlockSpec((tm, w), lambda i, c=c: (i, c))
    row = pl.BlockSpec((tm, LANES), lambda i: (i, 0))
    vec = pl.BlockSpec((1, LANES), lambda i: (0, 0))
    out = pl.BlockSpec((tm, w), lambda i: (i, 0))
    shp = jax.ShapeDtypeStruct((t, w), BF16)
    return pl.pallas_call(
        _dsa_prep_kernel,
        out_shape=(shp, shp, jax.ShapeDtypeStruct((w, t), BF16),
                   jax.ShapeDtypeStruct((t // DSA_SEL_TQ, IDX_HEADS * DSA_SEL_TQ, IDX_DIM), BF16),
                   jax.ShapeDtypeStruct((t, LANES), BF16), jax.ShapeDtypeStruct((t, IDX_HEADS), F32)),
        grid=(t // tm,),
        in_specs=[col(qblk), col(qblk + 1), col(qblk + 2),
                  pl.BlockSpec((tm, IDX_HEADS * IDX_DIM), lambda i: (i, qiblk)), row,
                  pl.BlockSpec((tm, IDX_HEADS), lambda i: (i, 0)), row, row, row, row, vec, vec],
        out_specs=(out, out, pl.BlockSpec((w, tm), lambda i: (0, i)),
                   pl.BlockSpec((nb, IDX_HEADS * DSA_SEL_TQ, IDX_DIM), lambda i: (i, 0, 0)), row,
                   pl.BlockSpec((tm, IDX_HEADS), lambda i: (i, 0))),
        compiler_params=_cparams("parallel"),
    )(proj, proj, proj, proj, ki, wi, *tabs64, *tabs128, qg.reshape(1, LANES), kg.reshape(1, LANES))


def _dsa_select_kernel(n_sel, qi_ref, ki_ref, wt_ref, mask_ref, keys_sc):
    i = pl.program_id(0)
    tq, tk = DSA_SEL_TQ, DSA_SEL_TK
    t = ki_ref.shape[0]
    nvalid = ((i + 1) * tq + tk - 1) // tk
    qpos = i * tq + lax.broadcasted_iota(jnp.int32, (1, tq), 1)
    w = wt_ref[...]
    q_all = qi_ref[0]
    chunk = lambda c: pl.ds(pl.multiple_of(c * tk, tk), tk)

    def score_chunk(c, carry):
        lg = lax.dot_general(ki_ref[chunk(c), :IDX_DIM], q_all, _NT, preferred_element_type=F32)
        acc = jnp.zeros((tk, tq), F32)
        for h in range(IDX_HEADS):
            acc = acc + w[h:h + 1, :] * jnp.maximum(lg[:, h * tq:(h + 1) * tq], 0.0)
        bits = pltpu.bitcast(acc, jnp.int32)
        key = jnp.where(bits < 0, bits ^ 0x7FFFFFFF, bits)
        kpos = c * tk + lax.broadcasted_iota(jnp.int32, (tk, 1), 0)
        keys_sc[chunk(c), :] = jnp.where(kpos <= qpos, key, INT_MIN)
        return carry

    lax.fori_loop(0, nvalid, score_chunk, 0)

    def count_ge(thr):
        def body(c, acc):
            hit = jnp.where(keys_sc[chunk(c), :] >= thr, 1, 0)
            return acc + jnp.sum(hit.reshape(tk // 8, 8, tq), axis=0)

        acc = lax.fori_loop(0, nvalid, body, jnp.zeros((8, tq), jnp.int32))
        return jnp.sum(acc, axis=0, keepdims=True)

    lo0 = jnp.full((1, tq), INT_MIN + 1, jnp.int32)
    hi0 = jnp.full((1, tq), 2 ** 31 - 1, jnp.int32)
    cnt0 = count_ge(lo0)

    def unsettled(lo, hi, cnt_lo):
        return jnp.max(jnp.where((cnt_lo > n_sel) & (lo + 1 != hi), 1, 0))

    def bisect(carry):
        lo, hi, cnt_lo, _ = carry
        mid = (lo >> 1) + (hi >> 1) + (lo & hi & 1)
        cnt = count_ge(mid)
        ok = cnt >= n_sel
        lo, hi, cnt_lo = jnp.where(ok, mid, lo), jnp.where(ok, hi, mid), jnp.where(ok, cnt, cnt_lo)
        return lo, hi, cnt_lo, unsettled(lo, hi, cnt_lo)

    lo, _, _, _ = lax.while_loop(lambda cr: cr[3] > 0, bisect, (lo0, hi0, cnt0, unsettled(lo0, hi0, cnt0)))

    def write_valid(c, carry):
        mask_ref[chunk(c), :] = jnp.where(keys_sc[chunk(c), :] >= lo, 1.0, 0.0).astype(mask_ref.dtype)
        return carry

    def write_zero(c, carry):
        mask_ref[chunk(c), :] = jnp.zeros((tk, tq), mask_ref.dtype)
        return carry

    lax.fori_loop(0, nvalid, write_valid, 0)
    lax.fori_loop(nvalid, t // tk, write_zero, 0)


def dsa_select(qi, ki, wt, n_sel):
    t = ki.shape[0]
    tq = DSA_SEL_TQ
    return pl.pallas_call(
        functools.partial(_dsa_select_kernel, n_sel),
        out_shape=jax.ShapeDtypeStruct((t, t), BF16),
        grid=(t // tq,),
        in_specs=[pl.BlockSpec((1, IDX_HEADS * tq, IDX_DIM), lambda i: (i, 0, 0)),
                  pl.BlockSpec((t, LANES), lambda i: (0, 0)),
                  pl.BlockSpec((IDX_HEADS, tq), lambda i: (0, i))],
        out_specs=pl.BlockSpec((t, tq), lambda i: (0, i)),
        scratch_shapes=[pltpu.VMEM((t, tq), jnp.int32)],
        compiler_params=_cparams("parallel"),
    )(qi, ki, wt)


def _dsa_flash_kernel(qi_of, kj_of, q_ref, k_ref, vt_ref, mask_ref, o_ref, m_sc, l_sc, acc_sc):
    step = pl.program_id(0)
    qi, kj = qi_of[step], kj_of[step]

    @pl.when(kj == 0)
    def _():
        _init_softmax_stats(m_sc, l_sc, acc_sc)

    def one_head(h, carry):
        c = pl.ds(pl.multiple_of(h * LANES, LANES), LANES)
        st = lax.dot_general(k_ref[:, c], q_ref[:, c], _NT, preferred_element_type=F32) * (DSA_DIM ** -0.5)
        st = jnp.where(mask_ref[...] > 0, st, NEG_BIG)
        _softmax_step_t(st, vt_ref[c, :], h, m_sc, l_sc, acc_sc)
        return carry

    lax.fori_loop(0, DSA_HEADS, one_head, 0)

    @pl.when(kj == qi)
    def _():
        def finish(h, carry):
            o_ref[:, pl.ds(pl.multiple_of(h * LANES, LANES), LANES)] = (acc_sc[h] / l_sc[h]).T.astype(o_ref.dtype)
            return carry

        lax.fori_loop(0, DSA_HEADS, finish, 0)


def dsa_flash(q, k, vt, mask_t):
    t = q.shape[0]
    tile = _tile(t, ATTN_TILE)
    qi_of, kj_of = _causal_steps(t, tile)
    grid_spec = pltpu.PrefetchScalarGridSpec(
        num_scalar_prefetch=2,
        grid=(qi_of.shape[0],),
        in_specs=[pl.BlockSpec((tile, DSA_WIDTH), lambda s, qi, kj: (qi[s], 0)),
                  pl.BlockSpec((tile, DSA_WIDTH), lambda s, qi, kj: (kj[s], 0)),
                  pl.BlockSpec((DSA_WIDTH, tile), lambda s, qi, kj: (0, kj[s])),
                  pl.BlockSpec((tile, tile), lambda s, qi, kj: (kj[s], qi[s]))],
        out_specs=pl.BlockSpec((tile, DSA_WIDTH), lambda s, qi, kj: (qi[s], 0)),
        scratch_shapes=[pltpu.VMEM((DSA_HEADS, 1, tile), F32), pltpu.VMEM((DSA_HEADS, 1, tile), F32),
                        pltpu.VMEM((DSA_HEADS, LANES, tile), F32)])
    return pl.pallas_call(
        _dsa_flash_kernel,
        out_shape=jax.ShapeDtypeStruct((t, DSA_WIDTH), BF16),
        grid_spec=grid_spec,
        compiler_params=_cparams("arbitrary"),
    )(qi_of, kj_of, q, k, vt, mask_t)


def dsa_attention(proj, qblk, qiblk, ki, wi, tabs64, tabs128, qg, kg):
    t = proj.shape[0]
    q, k, vt, qi, kir, wis = dsa_prep(proj, qblk, qiblk, ki, wi, tabs64, tabs128, qg, kg)
    mask_t = dsa_select(qi, kir, wis.T, min(TOPK_MAX, t // 4))
    return dsa_flash(q, k, vt, mask_t)


def _take_top(s, count):
    n = s.shape[0]
    idx = lax.broadcasted_iota(jnp.int32, s.shape, 0)
    vals = []
    for _ in range(count):
        m = jnp.max(s, axis=0, keepdims=True)
        first = jnp.min(jnp.where(s == m, idx, n), axis=0, keepdims=True)
        s = jnp.where(idx == first, NEG_BIG, s)
        vals.append(m)
    return vals


def _peer_route_kernel(q_ref, sk_ref, s1_ref, s2_ref, e1_ref, e2_ref, tau_ref):
    for h in range(PEER_HEADS):
        halves = []
        for c in range(2):
            sl = slice((2 * h + c) * LANES, (2 * h + c + 1) * LANES)
            halves.append(lax.dot_general(sk_ref[h, c], q_ref[:, sl].astype(BF16), _NT, preferred_element_type=F32))
        s1, s2 = halves
        v1, v2 = _take_top(s1, PEER_TOPK), _take_top(s2, PEER_TOPK)
        v2_all = jnp.concatenate(v2, axis=0)
        cand = jnp.concatenate([v1[a] + v2_all for a in range(PEER_TOPK)], axis=0)
        top = _take_top(cand, PEER_TOPK)
        z = sum(jnp.exp(tv - top[0]) for tv in top)
        s1_ref[h], s2_ref[h] = s1, s2
        e1_ref[h] = jnp.exp(s1 - v1[0])
        e2_ref[h] = jnp.exp(s2 - v2[0]) / z
        tau_ref[h:h + 1, :] = top[-1]


def peer_route(q, sub_keys):
    t = q.shape[0]
    tm = _tile(t, 256)
    big = pl.BlockSpec((PEER_HEADS, PEER_NKEYS, tm), lambda i: (0, 0, i))
    bs = jax.ShapeDtypeStruct((PEER_HEADS, PEER_NKEYS, t), F32)
    return pl.pallas_call(
        _peer_route_kernel,
        out_shape=(bs, bs, bs, bs, jax.ShapeDtypeStruct((PEER_HEADS, t), F32)),
        grid=(t // tm,),
        in_specs=[pl.BlockSpec((tm, PEER_HEADS * PEER_DKEY), lambda i: (i, 0)),
                  pl.BlockSpec(sub_keys.shape, lambda i: (0, 0, 0, 0))],
        out_specs=(big, big, big, big, pl.BlockSpec((PEER_HEADS, tm), lambda i: (0, i))),
        compiler_params=_cparams("parallel"),
    )(q, sub_keys)


def _gelu_tanh(x):
    return 0.5 * x * (1.0 + jnp.tanh(math.sqrt(2.0 / math.pi) * (x + 0.044715 * (x * x * x))))


def _peer_expert_kernel(x_ref, u_ref, vt_ref, s1_ref, s2_ref, e1_ref, e2_ref, tau_ref, o_ref):
    j = pl.program_id(1)
    te = u_ref.shape[0]

    @pl.when(j == 0)
    def _():
        o_ref[...] = jnp.zeros(o_ref.shape, F32)

    act = _gelu_tanh(lax.dot_general(u_ref[...], x_ref[...], _NT, preferred_element_type=F32))
    rows_per = te // PEER_NKEYS
    gates = []
    for a in range(rows_per):
        n1 = j * rows_per + a
        g = None
        for h in range(PEER_HEADS):
            pair = s1_ref[h, pl.ds(n1, 1), :] + s2_ref[h]
            gh = jnp.where(pair >= tau_ref[h:h + 1, :], e1_ref[h, pl.ds(n1, 1), :] * e2_ref[h], 0.0)
            g = gh if g is None else g + gh
        gates.append(g)
    ga = (jnp.concatenate(gates, axis=0) * act).astype(BF16)
    o_ref[...] += jnp.dot(vt_ref[...], ga, preferred_element_type=F32)


def peer_expert(x, u, vt, s1, s2, e1, e2, tau, tm=512, te=512):
    t, d = x.shape
    e = u.shape[0]
    tm, te = _tile(t, tm), _tile(e, te)
    once = pl.Buffered(1)
    big = pl.BlockSpec((PEER_HEADS, PEER_NKEYS, tm), lambda i, j: (0, 0, i), pipeline_mode=once)
    return pl.pallas_call(
        _peer_expert_kernel,
        out_shape=jax.ShapeDtypeStruct((d, t), F32),
        grid=(t // tm, e // te),
        in_specs=[pl.BlockSpec((tm, d), lambda i, j: (i, 0), pipeline_mode=once),
                  pl.BlockSpec((te, d), lambda i, j: (j, 0)),
                  pl.BlockSpec((d, te), lambda i, j: (0, j)),
                  big, big, big, big,
                  pl.BlockSpec((PEER_HEADS, tm), lambda i, j: (0, i), pipeline_mode=once)],
        out_specs=pl.BlockSpec((d, tm), lambda i, j: (0, i)),
        compiler_params=_cparams("parallel", "arbitrary"),
    )(x, u, vt, s1, s2, e1, e2, tau)


def _add_t_rmsnorm_kernel(h_ref, pt_ref, g_ref, ho_ref, ao_ref):
    x = h_ref[...] + pt_ref[...].T
    ho_ref[...] = x
    ao_ref[...] = (x * lax.rsqrt(jnp.mean(x * x, axis=-1, keepdims=True) + NORM_EPS) * g_ref[...]).astype(BF16)


def add_t_rmsnorm(h, pt, g):
    t, d = h.shape
    tm = _tile(t, 256)
    blk = pl.BlockSpec((tm, d), lambda i: (i, 0))
    return pl.pallas_call(
        _add_t_rmsnorm_kernel,
        out_shape=(jax.ShapeDtypeStruct((t, d), F32), jax.ShapeDtypeStruct((t, d), BF16)),
        grid=(t // tm,),
        in_specs=[blk, pl.BlockSpec((d, tm), lambda i: (0, i)), pl.BlockSpec((1, d), lambda i: (0, 0))],
        out_specs=(blk, blk),
        compiler_params=_cparams("parallel"),
    )(h, pt, g.reshape(1, d))


def _ple_kernel(a_ref, wg_ref, p_ref, wp_ref, h_ref, o_ref):
    gate = jax.nn.sigmoid(jnp.dot(a_ref[...], wg_ref[...], preferred_element_type=F32))
    o_ref[...] = h_ref[...] + gate * jnp.dot(p_ref[...], wp_ref[...], preferred_element_type=F32)


def ple(a, wg, p, wp, h, tm=512, tn=1024):
    m, k = a.shape
    n = wg.shape[1]
    kp = p.shape[1]
    tm, tn = _tile(m, tm), _tile(n, tn)
    return pl.pallas_call(
        _ple_kernel,
        out_shape=jax.ShapeDtypeStruct((m, n), F32),
        grid=(n // tn, m // tm),
        in_specs=[pl.BlockSpec((tm, k), lambda j, i: (i, 0)), pl.BlockSpec((k, tn), lambda j, i: (0, j)),
                  pl.BlockSpec((tm, kp), lambda j, i: (i, 0)), pl.BlockSpec((kp, tn), lambda j, i: (0, j)),
                  pl.BlockSpec((tm, tn), lambda j, i: (i, j))],
        out_specs=pl.BlockSpec((tm, tn), lambda j, i: (i, j)),
        compiler_params=_cparams("parallel", "parallel"),
    )(a, wg, p, wp, h)


_QBLK_GDN, _ZBLK_GDN = 0, 48
_BLK_DA = 8
_BLK_DSA = 11
_BLK_IDX = 7


def _regroup_w_in(w):
    sizes = (DA_WIDTH, DA_WIDTH, DA_WIDTH, 3 * GDN_WIDTH, GDN_WIDTH, GDN_HEADS, GDN_HEADS,
             DSA_WIDTH, DSA_WIDTH, DSA_WIDTH, IDX_HEADS * IDX_DIM, IDX_DIM, IDX_HEADS)
    offs = [0]
    for s in sizes:
        offs.append(offs[-1] + s)
    (da_q, da_k, da_v, g_qkv, g_z, g_b, g_a, c_q, c_k, c_v, c_qi, c_ki, c_w) = [
        w[:, offs[n]:offs[n + 1]] for n in range(len(sizes))]
    main = jnp.concatenate([g_qkv, g_z, da_q, da_k, da_v, c_q, c_k, c_v, c_qi], axis=1).astype(BF16)
    small = jnp.concatenate([c_ki, c_w, g_b, g_a], axis=1).astype(BF16)
    return main, small


def kernel(x, p, attn_norm, w_in, da_q_norm, da_k_norm, da_lambda, da_subln, gdn_conv, gdn_a_log, gdn_dt_bias, gdn_norm, dsa_q_norm, dsa_k_norm, w_out, ffn_norm, peer_w_q, peer_sub_keys, peer_u, peer_v, ple_norm, w_ple_gate, w_ple_proj):
    b, t, d = x.shape
    assert b == 1
    depth = w_in.shape[0]
    tabs64, tabs128 = _rope_tables(t, 64), _rope_tables(t, 128)
    h = x.reshape(t, d)
    for i in range(depth):
        w_main, w_small = _regroup_w_in(w_in[i])
        a = rmsnorm_bf16(h, attn_norm[i])
        proj = matmul(a, w_main)
        small = matmul(a, w_small)
        lam_init = 0.8 - 0.6 * math.exp(-0.3 * i)
        qa, ka, va = da_prep(proj, _BLK_DA, *tabs64, da_q_norm[i], da_k_norm[i])
        o_a = da_flash(qa, ka, va, da_lambda[i], da_subln[i], lam_init)
        o_b = gated_deltanet(proj, small[:, 96:112], small[:, 112:128], _QBLK_GDN, _ZBLK_GDN, gdn_conv[i],
                             gdn_a_log[i], gdn_dt_bias[i], gdn_norm[i])
        o_c = dsa_attention(proj, _BLK_DSA, _BLK_IDX, small, small[:, 64:96], tabs64, tabs128,
                            dsa_q_norm[i], dsa_k_norm[i])
        mix = jnp.concatenate([o_a, o_b, o_c], axis=-1)
        h = matmul(mix, w_out[i].astype(BF16), residual=h)

        a = rmsnorm_bf16(h, ffn_norm[i])
        pq = matmul(a, peer_w_q[i].astype(BF16))
        s1, s2, e1, e2, tau = peer_route(pq, peer_sub_keys[i].astype(BF16))
        pt = peer_expert(a, peer_u[i].astype(BF16), peer_v[i].T.astype(BF16), s1, s2, e1, e2, tau)
        h, a = add_t_rmsnorm(h, pt, ple_norm[i])
        h = ple(a, w_ple_gate[i].astype(BF16), p[i, 0].astype(BF16), w_ple_proj[i].astype(BF16), h)
    return h.reshape(b, t, d)
```

```python
import functools
import math

import jax
import jax.numpy as jnp
from jax import lax
from jax.experimental import pallas as pl
from jax.experimental.pallas import tpu as pltpu

F32 = jnp.float32
BF16 = jnp.bfloat16
HIGHEST = lax.Precision.HIGHEST

NORM_EPS = 1e-6
ROPE_THETA = 10000.0
LANES = 128
VMEM_LIMIT = 56 * 1024 * 1024

DA_HEADS, DA_DIM = 8, 64
DA_WIDTH = DA_HEADS * 2 * DA_DIM
GDN_HEADS, GDN_DK, GDN_DV, GDN_CONV, GDN_CHUNK = 16, 128, 128, 4, 64
GDN_WIDTH = GDN_HEADS * GDN_DV
DSA_HEADS, DSA_DIM = 8, 128
DSA_WIDTH = DSA_HEADS * DSA_DIM
IDX_HEADS, IDX_DIM = 32, 64
TOPK_MAX = 256
PEER_HEADS, PEER_NKEYS, PEER_DKEY, PEER_TOPK = 8, 128, 256, 16
PLE_DIM = 256

LOG2E = math.log2(math.e)
NEG_BIG = -1e30
INT_MIN = -(2 ** 31)
INT_MAX = 2 ** 31 - 1

_NT = (((1,), (1,)), ((), ()))


def _cparams(*sem):
    return pltpu.CompilerParams(dimension_semantics=sem, vmem_limit_bytes=VMEM_LIMIT)


def _tile(n, pref):
    t = min(n, pref)
    assert n % t == 0, (n, pref)
    return t


def _rmsnorm_kernel(x_ref, g_ref, o_ref):
    x = x_ref[...]
    y = x * lax.rsqrt(jnp.mean(x * x, axis=-1, keepdims=True) + NORM_EPS) * g_ref[...]
    o_ref[...] = y.astype(o_ref.dtype)


def rmsnorm_bf16(x, g):
    t, d = x.shape
    tm = _tile(t, 256)
    return pl.pallas_call(
        _rmsnorm_kernel,
        out_shape=jax.ShapeDtypeStruct((t, d), BF16),
        grid=(t // tm,),
        in_specs=[pl.BlockSpec((tm, d), lambda i: (i, 0)), pl.BlockSpec((1, d), lambda i: (0, 0))],
        out_specs=pl.BlockSpec((tm, d), lambda i: (i, 0)),
        compiler_params=_cparams("parallel"),
    )(x, g.reshape(1, d))


def _matmul_kernel(a_ref, b_ref, o_ref):
    o_ref[...] = jnp.dot(a_ref[...], b_ref[...], preferred_element_type=F32)


def _matmul_res_kernel(a_ref, b_ref, r_ref, o_ref):
    o_ref[...] = r_ref[...] + jnp.dot(a_ref[...], b_ref[...], preferred_element_type=F32)


def matmul(a, b, residual=None, tm=512, tn=1024):
    m, k = a.shape
    n = b.shape[1]
    tm, tn = _tile(m, tm), _tile(n, tn)
    in_specs = [pl.BlockSpec((tm, k), lambda j, i: (i, 0)), pl.BlockSpec((k, tn), lambda j, i: (0, j))]
    args = [a, b]
    body = _matmul_kernel
    if residual is not None:
        in_specs.append(pl.BlockSpec((tm, tn), lambda j, i: (i, j)))
        args.append(residual)
        body = _matmul_res_kernel
    return pl.pallas_call(
        body,
        out_shape=jax.ShapeDtypeStruct((m, n), F32),
        grid=(n // tn, m // tm),
        in_specs=in_specs,
        out_specs=pl.BlockSpec((tm, tn), lambda j, i: (i, j)),
        compiler_params=_cparams("parallel", "parallel"),
    )(*args)


def _rope_tables(t, d):
    pos = jnp.arange(t, dtype=F32)
    inv = ROPE_THETA ** (-jnp.arange(0, d, 2, dtype=F32) / d)
    ang = pos[:, None] * inv[None, :]
    cos, sin = jnp.cos(ang), jnp.sin(ang)
    reps = LANES // d
    return (jnp.tile(jnp.concatenate([cos, cos], -1), (1, reps)),
            jnp.tile(jnp.concatenate([-sin, sin], -1), (1, reps)))


def _rope64(y, cos, sin, lane):
    partner = jnp.where((lane & 32) == 0, pltpu.roll(y, 96, 1), pltpu.roll(y, 32, 1))
    return y * cos + partner * sin


def _rope128(y, cos, sin):
    return y * cos + pltpu.roll(y, 64, 1) * sin


def _seg64_mean_sq(x, lane):
    x2 = x * x
    lo = jnp.sum(jnp.where(lane < 64, x2, 0.0), axis=-1, keepdims=True)
    hi = jnp.sum(jnp.where(lane >= 64, x2, 0.0), axis=-1, keepdims=True)
    return jnp.where(lane < 64, lo, hi) * (1.0 / 64)


def _da_prep_kernel(q_ref, k_ref, v_ref, cos_ref, sin_ref, qg_ref, kg_ref, qo_ref, ko_ref, vo_ref):
    lane = lax.broadcasted_iota(jnp.int32, (1, LANES), 1)
    cos, sin = cos_ref[...], sin_ref[...]
    for c in range(DA_WIDTH // LANES):
        sl = slice(c * LANES, (c + 1) * LANES)
        for x_ref, g_ref, o_ref, scale in ((q_ref, qg_ref, qo_ref, DA_DIM ** -0.5 * LOG2E), (k_ref, kg_ref, ko_ref, 1.0)):
            x = x_ref[:, sl]
            y = x * lax.rsqrt(_seg64_mean_sq(x, lane) + NORM_EPS) * g_ref[...]
            y = _rope64(y, cos, sin, lane)
            o_ref[:, sl] = (y * scale).astype(BF16)
    vo_ref[...] = v_ref[...].T.astype(BF16)


def da_prep(proj, colblk, cos, sin, qg, kg):
    t = proj.shape[0]
    tm = _tile(t, 256)
    w = DA_WIDTH
    col = lambda c: pl.BlockSpec((tm, w), lambda i, c=c: (i, c))
    row = pl.BlockSpec((tm, LANES), lambda i: (i, 0))
    vec = pl.BlockSpec((1, LANES), lambda i: (0, 0))
    out = pl.BlockSpec((tm, w), lambda i: (i, 0))
    shp = jax.ShapeDtypeStruct((t, w), BF16)
    return pl.pallas_call(
        _da_prep_kernel,
        out_shape=(shp, shp, jax.ShapeDtypeStruct((w, t), BF16)),
        grid=(t // tm,),
        in_specs=[col(colblk), col(colblk + 1), col(colblk + 2), row, row, vec, vec],
        out_specs=(out, out, pl.BlockSpec((w, tm), lambda i: (0, i))),
        compiler_params=_cparams("parallel"),
    )(proj, proj, proj, cos, sin, jnp.tile(qg, 2).reshape(1, LANES), jnp.tile(kg, 2).reshape(1, LANES))


ATTN_TILE = 1024


def _causal_steps(t, tile):
    n = t // tile
    pairs = [(i, j) for i in range(n) for j in range(i + 1)]
    return (jnp.asarray([a for a, _ in pairs], jnp.int32), jnp.asarray([b for _, b in pairs], jnp.int32))


SUM_ROWS = 16


def _with_sum_rows(vt):
    return jnp.concatenate([vt, jnp.ones((SUM_ROWS, vt.shape[1]), vt.dtype)], axis=0)


def _softmax_step_t(st, vt1, idx, m_sc, acc_sc):
    m_prev = m_sc[idx]
    m_new = jnp.maximum(m_prev, jnp.max(st, axis=0, keepdims=True))
    p = jnp.exp2(st - m_new).astype(BF16)
    acc_sc[idx] = jnp.exp2(m_prev - m_new) * acc_sc[idx] + jnp.dot(vt1, p, preferred_element_type=F32)
    m_sc[idx] = m_new


def _softmax_result_t(acc):
    return acc[:LANES] / acc[LANES:LANES + 1]


def _init_softmax_stats(m_sc, acc_sc):
    m_sc[...] = jnp.full(m_sc.shape, NEG_BIG, F32)
    acc_sc[...] = jnp.zeros(acc_sc.shape, F32)


def _da_flash_kernel(lam_init, qi_of, kj_of, q_ref, k_ref, vt_ref, lp_ref, g_ref, o_ref, m_sc, acc_sc):
    step = pl.program_id(0)
    qi, kj = qi_of[step], kj_of[step]
    tq, tk = q_ref.shape[0], k_ref.shape[0]

    @pl.when(kj == 0)
    def _():
        _init_softmax_stats(m_sc, acc_sc)

    lane = lax.broadcasted_iota(jnp.int32, (1, LANES), 1)

    def all_heads(masked):
        def one_head(h, carry):
            c = pl.multiple_of(h * LANES, LANES)
            q, k, vt1 = q_ref[:, pl.ds(c, LANES)], k_ref[:, pl.ds(c, LANES)], _with_sum_rows(vt_ref[pl.ds(c, LANES), :])
            for mp in range(2):
                qm = jnp.where((lane < 64) == (mp == 0), q, jnp.zeros_like(q))
                st = lax.dot_general(k, qm, _NT, preferred_element_type=F32)
                if masked:
                    keep = (lax.broadcasted_iota(jnp.int32, (tk, tq), 0) <= lax.broadcasted_iota(jnp.int32, (tk, tq), 1))
                    st = jnp.where(keep, st, NEG_BIG)
                _softmax_step_t(st, vt1, 2 * h + mp, m_sc, acc_sc)
            return carry

        lax.fori_loop(0, DA_HEADS, one_head, 0)

    @pl.when(kj < qi)
    def _():
        all_heads(False)

    @pl.when(kj == qi)
    def _():
        all_heads(True)
        lp = lp_ref[...]
        lam = (jnp.exp(jnp.sum(lp[0:1] * lp[1:2], keepdims=True))
               - jnp.exp(jnp.sum(lp[2:3] * lp[3:4], keepdims=True)) + lam_init)

        def finish(h, carry):
            o = _softmax_result_t(acc_sc[2 * h]) - lam * _softmax_result_t(acc_sc[2 * h + 1])
            y = o * lax.rsqrt(jnp.mean(o * o, axis=0, keepdims=True) + NORM_EPS) * g_ref[...]
            o_ref[:, pl.ds(pl.multiple_of(h * LANES, LANES), LANES)] = (y * (1.0 - lam_init)).T.astype(o_ref.dtype)
            return carry

        lax.fori_loop(0, DA_HEADS, finish, 0)


def da_flash(q, k, vt, lam_params, subln_g, lam_init):
    t = q.shape[0]
    tile = _tile(t, ATTN_TILE)
    qi_of, kj_of = _causal_steps(t, tile)
    grid_spec = pltpu.PrefetchScalarGridSpec(
        num_scalar_prefetch=2,
        grid=(qi_of.shape[0],),
        in_specs=[pl.BlockSpec((tile, DA_WIDTH), lambda s, qi, kj: (qi[s], 0)),
                  pl.BlockSpec((tile, DA_WIDTH), lambda s, qi, kj: (kj[s], 0)),
                  pl.BlockSpec((DA_WIDTH, tile), lambda s, qi, kj: (0, kj[s])),
                  pl.BlockSpec((4, DA_DIM), lambda s, qi, kj: (0, 0)),
                  pl.BlockSpec((LANES, 1), lambda s, qi, kj: (0, 0))],
        out_specs=pl.BlockSpec((tile, DA_WIDTH), lambda s, qi, kj: (qi[s], 0)),
        scratch_shapes=[pltpu.VMEM((2 * DA_HEADS, 1, tile), F32),
                        pltpu.VMEM((2 * DA_HEADS, LANES + SUM_ROWS, tile), F32)])
    return pl.pallas_call(
        functools.partial(_da_flash_kernel, lam_init),
        out_shape=jax.ShapeDtypeStruct((t, DA_WIDTH), BF16),
        grid_spec=grid_spec,
        compiler_params=_cparams("arbitrary"),
    )(qi_of, kj_of, q, k, vt, lam_params, subln_g.reshape(LANES, 1))


GDN_GROUP = 256


def _softplus(x):
    return jnp.maximum(x, 0.0) + jnp.log1p(jnp.exp(-jnp.abs(x)))


def _gdn_gates_kernel(b_ref, a_ref, at_ref, alr_ref, dtr_ref, alc_ref, dtc_ref,
                      beta_ref, gcum_ref, grev_ref, gcumt_ref, egt_ref):
    n = b_ref.shape[0]
    r = lax.broadcasted_iota(jnp.int32, (n, n), 0)
    c = lax.broadcasted_iota(jnp.int32, (n, n), 1)
    same = (r // GDN_CHUNK) == (c // GDN_CHUNK)
    incl = jnp.where(same & (c <= r), 1.0, 0.0)
    rev = jnp.where(same & (c > r), 1.0, 0.0)
    beta_ref[...] = jax.nn.sigmoid(b_ref[...])
    g = -jnp.exp(alr_ref[...]) * _softplus(a_ref[...] + dtr_ref[...])
    gcum_ref[...] = jnp.dot(incl, g, precision=HIGHEST, preferred_element_type=F32)
    grev_ref[...] = jnp.dot(rev, g, precision=HIGHEST, preferred_element_type=F32)
    gt = -jnp.exp(alc_ref[...]) * _softplus(at_ref[...] + dtc_ref[...])
    gcumt = lax.dot_general(gt, incl, _NT, precision=HIGHEST, preferred_element_type=F32)
    gcumt_ref[...] = gcumt
    egt_ref[...] = jnp.exp(gcumt)


def gdn_gates(b_in, a_in, a_t, a_log, dt_bias):
    t, h = b_in.shape
    n = _tile(t, GDN_GROUP)
    colspec = pl.BlockSpec((n, h), lambda i: (i, 0))
    rowspec = pl.BlockSpec((h, n), lambda i: (0, i))
    vr = pl.BlockSpec((1, h), lambda i: (0, 0))
    vc = pl.BlockSpec((h, 1), lambda i: (0, 0))
    cs = jax.ShapeDtypeStruct((t, h), F32)
    rs = jax.ShapeDtypeStruct((h, t), F32)
    return pl.pallas_call(
        _gdn_gates_kernel,
        out_shape=(cs, cs, cs, rs, rs),
        grid=(t // n,),
        in_specs=[colspec, colspec, rowspec, vr, vr, vc, vc],
        out_specs=(colspec, colspec, colspec, rowspec, rowspec),
        compiler_params=_cparams("parallel"),
    )(b_in, a_in, a_t, a_log.reshape(1, h), dt_bias.reshape(1, h), a_log.reshape(h, 1), dt_bias.reshape(h, 1))


def _gdn_local_kernel(xq_ref, xk_ref, xv_ref, hq_ref, hk_ref, hv_ref, wq_ref, wk_ref, wv_ref,
                      beta_ref, gcum_ref, grev_ref, gcumt_ref,
                      u_ref, w_ref, qd_ref, kdt_ref, attn_ref):
    i, hg = pl.program_id(0), pl.program_id(1)
    n = xq_ref.shape[0]
    nh = xq_ref.shape[1] // LANES

    def conv_silu(x_ref, halo_ref, cw_ref, hl):
        x, cw = x_ref[:, hl], cw_ref[:, hl]
        halo = jnp.where(i > 0, halo_ref[:, hl], 0.0)

        def taps(z):
            return (cw[3:4] * z + cw[2:3] * pltpu.roll(z, 1, 0) + cw[1:2] * pltpu.roll(z, 2, 0)
                    + cw[0:1] * pltpu.roll(z, 3, 0))

        top = taps(jnp.concatenate([halo, x[:8]], axis=0))[8:16]
        y = jnp.concatenate([top, taps(x)[8:]], axis=0)
        return y * jax.nn.sigmoid(y)

    def l2n(y):
        return y * lax.rsqrt(jnp.sum(y * y, axis=-1, keepdims=True) + NORM_EPS)

    r = lax.broadcasted_iota(jnp.int32, (n, n), 0)
    c = lax.broadcasted_iota(jnp.int32, (n, n), 1)
    same = (r // GDN_CHUNK) == (c // GDN_CHUNK)
    heads = range(nh)
    hls = [slice(hb * LANES, (hb + 1) * LANES) for hb in heads]
    pws, tinvs, rest = [], [], []
    for hb in heads:
        hl, h = hls[hb], hg * nh + hb
        q = l2n(conv_silu(xq_ref, hq_ref, wq_ref, hl)) * (GDN_DK ** -0.5)
        k = l2n(conv_silu(xk_ref, hk_ref, wk_ref, hl))
        v = conv_silu(xv_ref, hv_ref, wv_ref, hl)
        head = lax.broadcasted_iota(jnp.int32, (1, GDN_HEADS), 1) == h
        pick = lambda ref: jnp.sum(jnp.where(head, ref[...], 0.0), axis=-1, keepdims=True)
        beta, gc, grev = pick(beta_ref), pick(gcum_ref), pick(grev_ref)
        gr = gcumt_ref[pl.ds(h, 1), :]
        decay = jnp.exp(jnp.where(same & (c <= r), gc - gr, NEG_BIG))
        kb = k * beta
        k16 = k.astype(BF16)
        kk = lax.dot_general(kb.astype(BF16), k16, _NT, preferred_element_type=F32)
        qk = lax.dot_general(q.astype(BF16), k16, _NT, preferred_element_type=F32)
        pw = -jnp.where(same & (c < r), kk * decay, 0.0)
        pws.append(pw)
        tinvs.append(jnp.where(r == c, 1.0, 0.0) + pw)
        attn = qk * decay
        qd_ref[:, hl] = (q * jnp.exp(gc)).astype(BF16)
        kdt_ref[hl, :] = (k * jnp.exp(grev)).T.astype(BF16)
        left = attn[:, :LANES]
        for j in range(1, n // LANES):
            left = left + attn[:, j * LANES:(j + 1) * LANES]
        attn_ref[:, hl] = (left + pltpu.roll(left, 64, 1)).astype(BF16)
        rest.append(((v * beta).astype(BF16), (kb * jnp.exp(gc)).astype(BF16)))

    for _ in range(5):
        for hb in heads:
            p16 = pws[hb].astype(BF16)
            pws[hb] = jnp.dot(p16, p16, preferred_element_type=F32)
            tinvs[hb] = tinvs[hb] + jnp.dot(tinvs[hb].astype(BF16), pws[hb].astype(BF16), preferred_element_type=F32)
    for hb in heads:
        t16 = tinvs[hb].astype(BF16)
        u_ref[:, hls[hb]] = jnp.dot(t16, rest[hb][0], preferred_element_type=F32)
        w_ref[:, hls[hb]] = jnp.dot(t16, rest[hb][1], preferred_element_type=F32).astype(BF16)


GDN_LOCAL_HEADS = 4


def gdn_local(proj, qblk, conv_w, beta, gcum, grev, gcumt):
    t = proj.shape[0]
    n = _tile(t, GDN_GROUP)
    nh, hl = GDN_HEADS, GDN_LOCAL_HEADS
    wide = hl * LANES
    assert qblk % hl == 0 and nh % hl == 0
    x = lambda part: pl.BlockSpec((n, wide), lambda i, h, part=part: (i, (qblk + part * nh) // hl + h))
    halo = lambda part: pl.BlockSpec(
        (8, wide), lambda i, h, part=part: (jnp.maximum(i * (n // 8) - 1, 0), (qblk + part * nh) // hl + h))
    cw = lambda part: pl.BlockSpec((GDN_CONV, wide), lambda i, h, part=part: (0, part * nh // hl + h))
    col = pl.BlockSpec((n, nh), lambda i, h: (i, 0))
    row = pl.BlockSpec((nh, n), lambda i, h: (0, i))
    out = pl.BlockSpec((n, wide), lambda i, h: (i, h))
    f32o = jax.ShapeDtypeStruct((t, GDN_WIDTH), F32)
    b16o = jax.ShapeDtypeStruct((t, GDN_WIDTH), BF16)
    return pl.pallas_call(
        _gdn_local_kernel,
        out_shape=(f32o, b16o, b16o, jax.ShapeDtypeStruct((GDN_WIDTH, t), BF16), b16o),
        grid=(t // n, nh // hl),
        in_specs=[x(0), x(1), x(2), halo(0), halo(1), halo(2), cw(0), cw(1), cw(2), col, col, col, row],
        out_specs=(out, out, out, pl.BlockSpec((wide, n), lambda i, h: (h, i)), out),
        compiler_params=_cparams("parallel", "parallel"),
    )(proj, proj, proj, proj, proj, proj, conv_w, conv_w, conv_w, beta, gcum, grev, gcumt)


GDN_SCAN_HEADS = 4


def _gdn_scan_kernel(egl_ref, u_ref, w_ref, qd_ref, kdt_ref, attn_ref, z_ref, g_ref, o_ref, s_ref):
    hg, i = pl.program_id(0), pl.program_id(1)
    n = u_ref.shape[0]
    cpg = n // GDN_CHUNK
    nh = s_ref.shape[0]

    @pl.when(i == 0)
    def _():
        s_ref[...] = jnp.zeros(s_ref.shape, F32)

    states = [s_ref[hb] for hb in range(nh)]
    zeros = jnp.zeros((GDN_CHUNK, GDN_DV), BF16)
    for c in range(cpg):
        sl = slice(c * GDN_CHUNK, (c + 1) * GDN_CHUNK)
        pair = slice((c // 2) * LANES, (c // 2 + 1) * LANES)
        for hb in range(nh):
            hl = slice(hb * LANES, (hb + 1) * LANES)
            s16 = states[hb].astype(BF16)
            vnew = u_ref[sl, hl] - jnp.dot(w_ref[sl, hl], s16, preferred_element_type=F32)
            v16 = vnew.astype(BF16)
            o = (jnp.dot(qd_ref[sl, hl], s16, preferred_element_type=F32)
                 + jnp.dot(attn_ref[sl, hb * LANES:hb * LANES + GDN_CHUNK], v16, preferred_element_type=F32))
            vpad = jnp.concatenate([v16, zeros] if c % 2 == 0 else [zeros, v16], axis=0)
            states[hb] = (states[hb] * egl_ref[hg * nh + hb, i * cpg + c]
                          + jnp.dot(kdt_ref[hl, pair], vpad, preferred_element_type=F32))
            y = o * lax.rsqrt(jnp.mean(o * o, axis=-1, keepdims=True) + NORM_EPS) * g_ref[...]
            z = z_ref[sl, hl]
            o_ref[sl, hl] = (y * (z * jax.nn.sigmoid(z))).astype(o_ref.dtype)
    for hb in range(nh):
        s_ref[hb] = states[hb]


def gdn_scan(egl, u, w, qd, kdt, attn, proj, zblk, norm_g):
    t = u.shape[0]
    n = _tile(t, GDN_GROUP)
    nh = GDN_SCAN_HEADS
    wide = nh * LANES
    assert zblk % nh == 0 and GDN_HEADS % nh == 0
    blk = pl.BlockSpec((n, wide), lambda h, i: (i, h))
    return pl.pallas_call(
        _gdn_scan_kernel,
        out_shape=jax.ShapeDtypeStruct((t, GDN_WIDTH), BF16),
        grid=(GDN_HEADS // nh, t // n),
        in_specs=[pl.BlockSpec(memory_space=pltpu.SMEM), blk, blk, blk,
                  pl.BlockSpec((wide, n), lambda h, i: (h, i)), blk,
                  pl.BlockSpec((n, wide), lambda h, i: (i, zblk // nh + h)),
                  pl.BlockSpec((1, LANES), lambda h, i: (0, 0))],
        out_specs=blk,
        scratch_shapes=[pltpu.VMEM((nh, GDN_DK, GDN_DV), F32)],
        compiler_params=_cparams("parallel", "arbitrary"),
    )(egl, u, w, qd, kdt, attn, proj, norm_g.reshape(1, LANES))


def gated_deltanet(proj, small_b, small_a, qblk, zblk, conv_w, a_log, dt_bias, norm_g):
    beta, gcum, grev, gcumt, egt = gdn_gates(small_b, small_a, small_a.T, a_log, dt_bias)
    u, w, qd, kdt, attn = gdn_local(proj, qblk, conv_w, beta, gcum, grev, gcumt)
    egl = egt[:, GDN_CHUNK - 1::GDN_CHUNK]
    return gdn_scan(egl, u, w, qd, kdt, attn, proj, zblk, norm_g)


DSA_SEL_TQ = 128
DSA_SEL_TK = 256


def _dsa_prep_kernel(q_ref, k_ref, v_ref, qi_ref, ki_ref, wi_ref, cos64_ref, sin64_ref, cos128_ref, sin128_ref,
                     qg_ref, kg_ref, qo_ref, ko_ref, vo_ref, qio_ref, kio_ref, wio_ref):
    lane = lax.broadcasted_iota(jnp.int32, (1, LANES), 1)
    cos64, sin64 = cos64_ref[...], sin64_ref[...]
    cos128, sin128 = cos128_ref[...], sin128_ref[...]
    for c in range(DSA_HEADS):
        sl = slice(c * LANES, (c + 1) * LANES)
        for x_ref, g_ref, o_ref, scale in ((q_ref, qg_ref, qo_ref, DSA_DIM ** -0.5 * LOG2E), (k_ref, kg_ref, ko_ref, 1.0)):
            x = x_ref[:, sl]
            y = x * lax.rsqrt(jnp.mean(x * x, axis=-1, keepdims=True) + NORM_EPS) * g_ref[...]
            o_ref[:, sl] = (_rope128(y, cos128, sin128) * scale).astype(BF16)
    vo_ref[...] = v_ref[...].T.astype(BF16)
    tq = DSA_SEL_TQ
    for c in range(IDX_HEADS * IDX_DIM // LANES):
        y = _rope64(qi_ref[:, c * LANES:(c + 1) * LANES], cos64, sin64, lane).astype(BF16)
        for b in range(qi_ref.shape[0] // tq):
            for half in range(2):
                h = 2 * c + half
                qio_ref[b, h * tq:(h + 1) * tq, :] = y[b * tq:(b + 1) * tq, half * IDX_DIM:(half + 1) * IDX_DIM]
    kio_ref[...] = _rope64(ki_ref[...], cos64, sin64, lane).astype(BF16)
    wio_ref[...] = wi_ref[...] * ((IDX_HEADS * IDX_DIM) ** -0.5)


def dsa_prep(proj, qblk, qiblk, ki, wi, tabs64, tabs128, qg, kg):
    t = proj.shape[0]
    tm = _tile(t, 256)
    w = DSA_WIDTH
    nb = tm // DSA_SEL_TQ
    col = lambda c: pl.BlockSpec((tm, w), lambda i, c=c: (i, c))
    row = pl.BlockSpec((tm, LANES), lambda i: (i, 0))
    vec = pl.BlockSpec((1, LANES), lambda i: (0, 0))
    out = pl.BlockSpec((tm, w), lambda i: (i, 0))
    shp = jax.ShapeDtypeStruct((t, w), BF16)
    return pl.pallas_call(
        _dsa_prep_kernel,
        out_shape=(shp, shp, jax.ShapeDtypeStruct((w, t), BF16),
                   jax.ShapeDtypeStruct((t // DSA_SEL_TQ, IDX_HEADS * DSA_SEL_TQ, IDX_DIM), BF16),
                   jax.ShapeDtypeStruct((t, LANES), BF16), jax.ShapeDtypeStruct((t, IDX_HEADS), F32)),
        grid=(t // tm,),
        in_specs=[col(qblk), col(qblk + 1), col(qblk + 2),
                  pl.BlockSpec((tm, IDX_HEADS * IDX_DIM), lambda i: (i, qiblk)), row,
                  pl.BlockSpec((tm, IDX_HEADS), lambda i: (i, 0)), row, row, row, row, vec, vec],
        out_specs=(out, out, pl.BlockSpec((w, tm), lambda i: (0, i)),
                   pl.BlockSpec((nb, IDX_HEADS * DSA_SEL_TQ, IDX_DIM), lambda i: (i, 0, 0)), row,
                   pl.BlockSpec((tm, IDX_HEADS), lambda i: (i, 0))),
        compiler_params=_cparams("parallel"),
    )(proj, proj, proj, proj, ki, wi, *tabs64, *tabs128, qg.reshape(1, LANES), kg.reshape(1, LANES))


def _order_key(x):
    bits = pltpu.bitcast(x, jnp.int32)
    return jnp.where(bits < 0, bits ^ 0x7FFFFFFF, bits)


def _order_value(key):
    return pltpu.bitcast(jnp.where(key < 0, key ^ 0x7FFFFFFF, key), F32)


def _dsa_select_kernel(n_sel, qi_ref, ki_ref, wt_ref, mask_ref, keys_sc):
    i = pl.program_id(0)
    tq, tk = DSA_SEL_TQ, DSA_SEL_TK
    t = ki_ref.shape[0]
    nvalid = ((i + 1) * tq + tk - 1) // tk
    qpos = i * tq + lax.broadcasted_iota(jnp.int32, (1, tq), 1)
    w = wt_ref[...]
    q_all = qi_ref[0]
    chunk = lambda c: pl.ds(pl.multiple_of(c * tk, tk), tk)

    def score_chunk(c, carry):
        lg = lax.dot_general(ki_ref[chunk(c), :IDX_DIM], q_all, _NT, preferred_element_type=F32)
        acc = jnp.zeros((tk, tq), F32)
        for h in range(IDX_HEADS):
            acc = acc + w[h:h + 1, :] * jnp.maximum(lg[:, h * tq:(h + 1) * tq], 0.0)
        key = _order_key(acc)
        kpos = c * tk + lax.broadcasted_iota(jnp.int32, (tk, 1), 0)
        causal = kpos <= qpos
        keys_sc[chunk(c), :] = jnp.where(causal, key, INT_MIN)
        kmin8, kmax8, sum8, sq8 = carry
        fold = lambda x: x.reshape(tk // 8, 8, tq)
        live = jnp.where(causal, acc, 0.0)
        return (jnp.minimum(kmin8, jnp.min(fold(jnp.where(causal, key, INT_MAX)), axis=0)),
                jnp.maximum(kmax8, jnp.max(fold(jnp.where(causal, key, INT_MIN)), axis=0)),
                sum8 + jnp.sum(fold(live), axis=0), sq8 + jnp.sum(fold(live * live), axis=0))

    zeros8 = jnp.zeros((8, tq), F32)
    kmin8, kmax8, sum8, sq8 = lax.fori_loop(0, nvalid, score_chunk, (
        jnp.full((8, tq), INT_MAX, jnp.int32), jnp.full((8, tq), INT_MIN, jnp.int32), zeros8, zeros8))

    def count_ge(thr):
        def body(c, acc):
            hit = jnp.where(keys_sc[chunk(c), :] >= thr, 1, 0)
            return acc + jnp.sum(hit.reshape(tk // 8, 8, tq), axis=0)

        acc = lax.fori_loop(0, nvalid, body, jnp.zeros((8, tq), jnp.int32))
        return jnp.sum(acc, axis=0, keepdims=True)

    lo0 = jnp.min(kmin8, axis=0, keepdims=True)
    hi0 = jnp.max(kmax8, axis=0, keepdims=True) + 1
    cnt_lo0 = qpos + 1
    cnt_hi0 = jnp.zeros((1, tq), jnp.int32)

    def unsettled(lo, hi, cnt_lo):
        return jnp.max(jnp.where((cnt_lo > n_sel) & (lo + 1 != hi), 1, 0))

    n_live = cnt_lo0.astype(F32)
    mean = jnp.sum(sum8, axis=0, keepdims=True) / n_live
    dev = jnp.sqrt(jnp.maximum(jnp.sum(sq8, axis=0, keepdims=True) / n_live - mean * mean, 0.0))
    tail = jnp.clip(n_sel / n_live, 1e-6, 0.5)
    tt = jnp.sqrt(-2.0 * jnp.log(tail))
    z = tt - (2.30753 + 0.27061 * tt) / (1.0 + 0.99229 * tt + 0.04481 * tt * tt)
    slope = dev / (z + 1.0 / jnp.maximum(z, 0.5))
    log_n = math.log(n_sel)

    def narrow(carry):
        lo, hi, cnt_lo, cnt_hi, w_lo, w_hi, last_ok, it, _ = carry
        half = (lo >> 1) + (hi >> 1) + (lo & hi & 1)
        v_lo, v_hi = _order_value(lo), _order_value(hi)
        e_lo, e_hi = jnp.log(cnt_lo.astype(F32) + 0.5) - log_n, jnp.log(cnt_hi.astype(F32) + 0.5) - log_n
        f_lo, f_hi = e_lo * w_lo, e_hi * w_hi
        false_pos = v_lo + f_lo / (f_lo - f_hi) * (v_hi - v_lo)
        along_tail = jnp.where(last_ok > 0, v_lo + e_lo * slope, v_hi + e_hi * slope)
        guess = _order_key(jnp.where(it == 0, mean + z * dev, jnp.where(it == 1, along_tail, false_pos)))
        mid = jnp.where((guess > lo) & (guess < hi) & ((it & 7) != 7), guess, half)
        cnt = count_ge(mid)
        ok = cnt >= n_sel
        w_hi = jnp.where(ok, jnp.where(last_ok > 0, 0.5 * w_hi, w_hi), 1.0)
        w_lo = jnp.where(ok, 1.0, jnp.where(last_ok < 0, 0.5 * w_lo, w_lo))
        lo, cnt_lo = jnp.where(ok, mid, lo), jnp.where(ok, cnt, cnt_lo)
        hi, cnt_hi = jnp.where(ok, hi, mid), jnp.where(ok, cnt_hi, cnt)
        return lo, hi, cnt_lo, cnt_hi, w_lo, w_hi, jnp.where(ok, 1, -1), it + 1, unsettled(lo, hi, cnt_lo)

    ones = jnp.ones((1, tq), F32)
    lo = lax.while_loop(lambda cr: cr[8] > 0, narrow,
                        (lo0, hi0, cnt_lo0, cnt_hi0, ones, ones, jnp.zeros((1, tq), jnp.int32), jnp.int32(0),
                         unsettled(lo0, hi0, cnt_lo0)))[0]

    def write_valid(c, carry):
        mask_ref[chunk(c), :] = jnp.where(keys_sc[chunk(c), :] >= lo, 1.0, 0.0).astype(mask_ref.dtype)
        return carry

    def write_zero(c, carry):
        mask_ref[chunk(c), :] = jnp.zeros((tk, tq), mask_ref.dtype)
        return carry

    lax.fori_loop(0, nvalid, write_valid, 0)
    lax.fori_loop(nvalid, t // tk, write_zero, 0)


def dsa_select(qi, ki, wt, n_sel):
    t = ki.shape[0]
    tq = DSA_SEL_TQ
    return pl.pallas_call(
        functools.partial(_dsa_select_kernel, n_sel),
        out_shape=jax.ShapeDtypeStruct((t, t), BF16),
        grid=(t // tq,),
        in_specs=[pl.BlockSpec((1, IDX_HEADS * tq, IDX_DIM), lambda i: (i, 0, 0)),
                  pl.BlockSpec((t, LANES), lambda i: (0, 0)),
                  pl.BlockSpec((IDX_HEADS, tq), lambda i: (0, i))],
        out_specs=pl.BlockSpec((t, tq), lambda i: (0, i)),
        scratch_shapes=[pltpu.VMEM((t, tq), jnp.int32)],
        compiler_params=_cparams("parallel"),
    )(qi, ki, wt)


def _dsa_flash_kernel(qi_of, kj_of, q_ref, k_ref, vt_ref, mask_ref, o_ref, m_sc, acc_sc):
    step = pl.program_id(0)
    qi, kj = qi_of[step], kj_of[step]

    @pl.when(kj == 0)
    def _():
        _init_softmax_stats(m_sc, acc_sc)

    def one_head(h, carry):
        c = pl.ds(pl.multiple_of(h * LANES, LANES), LANES)
        st = lax.dot_general(k_ref[:, c], q_ref[:, c], _NT, preferred_element_type=F32)
        st = jnp.where(mask_ref[...] > 0, st, NEG_BIG)
        _softmax_step_t(st, _with_sum_rows(vt_ref[c, :]), h, m_sc, acc_sc)
        return carry

    lax.fori_loop(0, DSA_HEADS, one_head, 0)

    @pl.when(kj == qi)
    def _():
        def finish(h, carry):
            o_ref[:, pl.ds(pl.multiple_of(h * LANES, LANES), LANES)] = _softmax_result_t(acc_sc[h]).T.astype(o_ref.dtype)
            return carry

        lax.fori_loop(0, DSA_HEADS, finish, 0)


def dsa_flash(q, k, vt, mask_t):
    t = q.shape[0]
    tile = _tile(t, ATTN_TILE)
    qi_of, kj_of = _causal_steps(t, tile)
    grid_spec = pltpu.PrefetchScalarGridSpec(
        num_scalar_prefetch=2,
        grid=(qi_of.shape[0],),
        in_specs=[pl.BlockSpec((tile, DSA_WIDTH), lambda s, qi, kj: (qi[s], 0)),
                  pl.BlockSpec((tile, DSA_WIDTH), lambda s, qi, kj: (kj[s], 0)),
                  pl.BlockSpec((DSA_WIDTH, tile), lambda s, qi, kj: (0, kj[s])),
                  pl.BlockSpec((tile, tile), lambda s, qi, kj: (kj[s], qi[s]))],
        out_specs=pl.BlockSpec((tile, DSA_WIDTH), lambda s, qi, kj: (qi[s], 0)),
        scratch_shapes=[pltpu.VMEM((DSA_HEADS, 1, tile), F32),
                        pltpu.VMEM((DSA_HEADS, LANES + SUM_ROWS, tile), F32)])
    return pl.pallas_call(
        _dsa_flash_kernel,
        out_shape=jax.ShapeDtypeStruct((t, DSA_WIDTH), BF16),
        grid_spec=grid_spec,
        compiler_params=_cparams("arbitrary"),
    )(qi_of, kj_of, q, k, vt, mask_t)


def dsa_attention(proj, qblk, qiblk, ki, wi, tabs64, tabs128, qg, kg):
    t = proj.shape[0]
    q, k, vt, qi, kir, wis = dsa_prep(proj, qblk, qiblk, ki, wi, tabs64, tabs128, qg, kg)
    mask_t = dsa_select(qi, kir, wis.T, min(TOPK_MAX, t // 4))
    return dsa_flash(q, k, vt, mask_t)


def _take_top(s, count):
    n = s.shape[0]
    idx = lax.broadcasted_iota(jnp.int32, s.shape, 0)
    vals = []
    for _ in range(count):
        m = jnp.max(s, axis=0, keepdims=True)
        first = jnp.min(jnp.where(s == m, idx, n), axis=0, keepdims=True)
        s = jnp.where(idx == first, NEG_BIG, s)
        vals.append(m)
    return vals


def _peer_route_kernel(q_ref, sk_ref, s1_ref, s2_ref, e1_ref, e2_ref, tau_ref):
    for h in range(PEER_HEADS):
        halves = []
        for c in range(2):
            sl = slice((2 * h + c) * LANES, (2 * h + c + 1) * LANES)
            halves.append(lax.dot_general(sk_ref[h, c], q_ref[:, sl].astype(BF16), _NT, preferred_element_type=F32))
        s1, s2 = halves
        v1, v2 = _take_top(s1, PEER_TOPK), _take_top(s2, PEER_TOPK)
        v2_all = jnp.concatenate(v2, axis=0)
        cand = jnp.concatenate([v1[a] + v2_all for a in range(PEER_TOPK)], axis=0)
        top = _take_top(cand, PEER_TOPK)
        z = sum(jnp.exp(tv - top[0]) for tv in top)
        s1_ref[h], s2_ref[h] = s1, s2
        e1_ref[h] = jnp.exp(s1 - v1[0])
        e2_ref[h] = jnp.exp(s2 - v2[0]) / z
        tau_ref[h:h + 1, :] = top[-1]


def peer_route(q, sub_keys):
    t = q.shape[0]
    tm = _tile(t, 256)
    big = pl.BlockSpec((PEER_HEADS, PEER_NKEYS, tm), lambda i: (0, 0, i))
    bs = jax.ShapeDtypeStruct((PEER_HEADS, PEER_NKEYS, t), F32)
    return pl.pallas_call(
        _peer_route_kernel,
        out_shape=(bs, bs, bs, bs, jax.ShapeDtypeStruct((PEER_HEADS, t), F32)),
        grid=(t // tm,),
        in_specs=[pl.BlockSpec((tm, PEER_HEADS * PEER_DKEY), lambda i: (i, 0)),
                  pl.BlockSpec(sub_keys.shape, lambda i: (0, 0, 0, 0))],
        out_specs=(big, big, big, big, pl.BlockSpec((PEER_HEADS, tm), lambda i: (0, i))),
        compiler_params=_cparams("parallel"),
    )(q, sub_keys)


def _gelu_tanh(x):
    return 0.5 * x * (1.0 + jnp.tanh(math.sqrt(2.0 / math.pi) * (x + 0.044715 * (x * x * x))))


def _peer_expert_kernel(x_ref, u_ref, vt_ref, s1_ref, s2_ref, e1_ref, e2_ref, tau_ref, o_ref):
    j = pl.program_id(1)
    te = u_ref.shape[0]

    @pl.when(j == 0)
    def _():
        o_ref[...] = jnp.zeros(o_ref.shape, F32)

    act = _gelu_tanh(lax.dot_general(u_ref[...], x_ref[...], _NT, preferred_element_type=F32))
    rows_per = te // PEER_NKEYS
    gates = []
    for a in range(rows_per):
        n1 = j * rows_per + a
        g = None
        for h in range(PEER_HEADS):
            pair = s1_ref[h, pl.ds(n1, 1), :] + s2_ref[h]
            gh = jnp.where(pair >= tau_ref[h:h + 1, :], e1_ref[h, pl.ds(n1, 1), :] * e2_ref[h], 0.0)
            g = gh if g is None else g + gh
        gates.append(g)
    ga = (jnp.concatenate(gates, axis=0) * act).astype(BF16)
    o_ref[...] += jnp.dot(vt_ref[...], ga, preferred_element_type=F32)


def peer_expert(x, u, vt, s1, s2, e1, e2, tau, tm=512, te=512):
    t, d = x.shape
    e = u.shape[0]
    tm, te = _tile(t, tm), _tile(e, te)
    once = pl.Buffered(1)
    big = pl.BlockSpec((PEER_HEADS, PEER_NKEYS, tm), lambda i, j: (0, 0, i), pipeline_mode=once)
    return pl.pallas_call(
        _peer_expert_kernel,
        out_shape=jax.ShapeDtypeStruct((d, t), F32),
        grid=(t // tm, e // te),
        in_specs=[pl.BlockSpec((tm, d), lambda i, j: (i, 0), pipeline_mode=once),
                  pl.BlockSpec((te, d), lambda i, j: (j, 0)),
                  pl.BlockSpec((d, te), lambda i, j: (0, j)),
                  big, big, big, big,
                  pl.BlockSpec((PEER_HEADS, tm), lambda i, j: (0, i), pipeline_mode=once)],
        out_specs=pl.BlockSpec((d, tm), lambda i, j: (0, i)),
        compiler_params=_cparams("parallel", "arbitrary"),
    )(x, u, vt, s1, s2, e1, e2, tau)


def _add_t_rmsnorm_kernel(h_ref, pt_ref, g_ref, ho_ref, ao_ref):
    x = h_ref[...] + pt_ref[...].T
    ho_ref[...] = x
    ao_ref[...] = (x * lax.rsqrt(jnp.mean(x * x, axis=-1, keepdims=True) + NORM_EPS) * g_ref[...]).astype(BF16)


def add_t_rmsnorm(h, pt, g):
    t, d = h.shape
    tm = _tile(t, 256)
    blk = pl.BlockSpec((tm, d), lambda i: (i, 0))
    return pl.pallas_call(
        _add_t_rmsnorm_kernel,
        out_shape=(jax.ShapeDtypeStruct((t, d), F32), jax.ShapeDtypeStruct((t, d), BF16)),
        grid=(t // tm,),
        in_specs=[blk, pl.BlockSpec((d, tm), lambda i: (0, i)), pl.BlockSpec((1, d), lambda i: (0, 0))],
        out_specs=(blk, blk),
        compiler_params=_cparams("parallel"),
    )(h, pt, g.reshape(1, d))


def _ple_kernel(a_ref, wg_ref, p_ref, wp_ref, h_ref, o_ref):
    gate = jax.nn.sigmoid(jnp.dot(a_ref[...], wg_ref[...], preferred_element_type=F32))
    o_ref[...] = h_ref[...] + gate * jnp.dot(p_ref[...], wp_ref[...], preferred_element_type=F32)


def ple(a, wg, p, wp, h, tm=512, tn=1024):
    m, k = a.shape
    n = wg.shape[1]
    kp = p.shape[1]
    tm, tn = _tile(m, tm), _tile(n, tn)
    return pl.pallas_call(
        _ple_kernel,
        out_shape=jax.ShapeDtypeStruct((m, n), F32),
        grid=(n // tn, m // tm),
        in_specs=[pl.BlockSpec((tm, k), lambda j, i: (i, 0)), pl.BlockSpec((k, tn), lambda j, i: (0, j)),
                  pl.BlockSpec((tm, kp), lambda j, i: (i, 0)), pl.BlockSpec((kp, tn), lambda j, i: (0, j)),
                  pl.BlockSpec((tm, tn), lambda j, i: (i, j))],
        out_specs=pl.BlockSpec((tm, tn), lambda j, i: (i, j)),
        compiler_params=_cparams("parallel", "parallel"),
    )(a, wg, p, wp, h)


_QBLK_GDN, _ZBLK_GDN = 0, 48
_BLK_DA = 8
_BLK_DSA = 11
_BLK_IDX = 7


def _regroup_w_in(w):
    sizes = (DA_WIDTH, DA_WIDTH, DA_WIDTH, 3 * GDN_WIDTH, GDN_WIDTH, GDN_HEADS, GDN_HEADS,
             DSA_WIDTH, DSA_WIDTH, DSA_WIDTH, IDX_HEADS * IDX_DIM, IDX_DIM, IDX_HEADS)
    offs = [0]
    for s in sizes:
        offs.append(offs[-1] + s)
    (da_q, da_k, da_v, g_qkv, g_z, g_b, g_a, c_q, c_k, c_v, c_qi, c_ki, c_w) = [
        w[:, offs[n]:offs[n + 1]] for n in range(len(sizes))]
    main = jnp.concatenate([g_qkv, g_z, da_q, da_k, da_v, c_q, c_k, c_v, c_qi], axis=1).astype(BF16)
    small = jnp.concatenate([c_ki, c_w, g_b, g_a], axis=1).astype(BF16)
    return main, small


def kernel(x, p, attn_norm, w_in, da_q_norm, da_k_norm, da_lambda, da_subln, gdn_conv, gdn_a_log, gdn_dt_bias, gdn_norm, dsa_q_norm, dsa_k_norm, w_out, ffn_norm, peer_w_q, peer_sub_keys, peer_u, peer_v, ple_norm, w_ple_gate, w_ple_proj):
    b, t, d = x.shape
    assert b == 1
    depth = w_in.shape[0]
    tabs64, tabs128 = _rope_tables(t, 64), _rope_tables(t, 128)
    h = x.reshape(t, d)
    for i in range(depth):
        w_main, w_small = _regroup_w_in(w_in[i])
        a = rmsnorm_bf16(h, attn_norm[i])
        proj = matmul(a, w_main)
        small = matmul(a, w_small)
        lam_init = 0.8 - 0.6 * math.exp(-0.3 * i)
        qa, ka, va = da_prep(proj, _BLK_DA, *tabs64, da_q_norm[i], da_k_norm[i])
        o_a = da_flash(qa, ka, va, da_lambda[i], da_subln[i], lam_init)
        o_b = gated_deltanet(proj, small[:, 96:112], small[:, 112:128], _QBLK_GDN, _ZBLK_GDN, gdn_conv[i],
                             gdn_a_log[i], gdn_dt_bias[i], gdn_norm[i])
        o_c = dsa_attention(proj, _BLK_DSA, _BLK_IDX, small, small[:, 64:96], tabs64, tabs128,
                            dsa_q_norm[i], dsa_k_norm[i])
        mix = jnp.concatenate([o_a, o_b, o_c], axis=-1)
        h = matmul(mix, w_out[i].astype(BF16), residual=h)

        a = rmsnorm_bf16(h, ffn_norm[i])
        pq = matmul(a, peer_w_q[i].astype(BF16))
        s1, s2, e1, e2, tau = peer_route(pq, peer_sub_keys[i].astype(BF16))
        pt = peer_expert(a, peer_u[i].astype(BF16), peer_v[i].T.astype(BF16), s1, s2, e1, e2, tau)
        h, a = add_t_rmsnorm(h, pt, ple_norm[i])
        h = ple(a, w_ple_gate[i].astype(BF16), p[i, 0].astype(BF16), w_ple_proj[i].astype(BF16), h)
    return h.reshape(b, t, d)
```

```python
import functools
import math

import jax
import jax.numpy as jnp
from jax import lax
from jax.experimental import pallas as pl
from jax.experimental.pallas import tpu as pltpu

F32 = jnp.float32
BF16 = jnp.bfloat16
HIGHEST = lax.Precision.HIGHEST

NORM_EPS = 1e-6
ROPE_THETA = 10000.0
LANES = 128
VMEM_LIMIT = 56 * 1024 * 1024

DA_HEADS, DA_DIM = 8, 64
DA_WIDTH = DA_HEADS * 2 * DA_DIM
GDN_HEADS, GDN_DK, GDN_DV, GDN_CONV, GDN_CHUNK = 16, 128, 128, 4, 64
GDN_WIDTH = GDN_HEADS * GDN_DV
DSA_HEADS, DSA_DIM = 8, 128
DSA_WIDTH = DSA_HEADS * DSA_DIM
IDX_HEADS, IDX_DIM = 32, 64
TOPK_MAX = 256
PEER_HEADS, PEER_NKEYS, PEER_DKEY, PEER_TOPK = 8, 128, 256, 16
PLE_DIM = 256

LOG2E = math.log2(math.e)
NEG_BIG = -1e30
INT_MIN = -(2 ** 31)
INT_MAX = 2 ** 31 - 1

_NT = (((1,), (1,)), ((), ()))


def _cparams(*sem):
    return pltpu.CompilerParams(dimension_semantics=sem, vmem_limit_bytes=VMEM_LIMIT)


def _tile(n, pref):
    t = min(n, pref)
    assert n % t == 0, (n, pref)
    return t


def _rmsnorm_kernel(x_ref, g_ref, o_ref):
    x = x_ref[...]
    y = x * lax.rsqrt(jnp.mean(x * x, axis=-1, keepdims=True) + NORM_EPS) * g_ref[...]
    o_ref[...] = y.astype(o_ref.dtype)


def rmsnorm_bf16(x, g):
    t, d = x.shape
    tm = _tile(t, 256)
    return pl.pallas_call(
        _rmsnorm_kernel,
        out_shape=jax.ShapeDtypeStruct((t, d), BF16),
        grid=(t // tm,),
        in_specs=[pl.BlockSpec((tm, d), lambda i: (i, 0)), pl.BlockSpec((1, d), lambda i: (0, 0))],
        out_specs=pl.BlockSpec((tm, d), lambda i: (i, 0)),
        compiler_params=_cparams("parallel"),
    )(x, g.reshape(1, d))


def _matmul_kernel(a_ref, b_ref, o_ref):
    o_ref[...] = jnp.dot(a_ref[...], b_ref[...], preferred_element_type=F32)


def _matmul_res_kernel(a_ref, b_ref, r_ref, o_ref):
    o_ref[...] = r_ref[...] + jnp.dot(a_ref[...], b_ref[...], preferred_element_type=F32)


def matmul(a, b, residual=None, tm=512, tn=1024):
    m, k = a.shape
    n = b.shape[1]
    tm, tn = _tile(m, tm), _tile(n, tn)
    in_specs = [pl.BlockSpec((tm, k), lambda j, i: (i, 0)), pl.BlockSpec((k, tn), lambda j, i: (0, j))]
    args = [a, b]
    body = _matmul_kernel
    if residual is not None:
        in_specs.append(pl.BlockSpec((tm, tn), lambda j, i: (i, j)))
        args.append(residual)
        body = _matmul_res_kernel
    return pl.pallas_call(
        body,
        out_shape=jax.ShapeDtypeStruct((m, n), F32),
        grid=(n // tn, m // tm),
        in_specs=in_specs,
        out_specs=pl.BlockSpec((tm, tn), lambda j, i: (i, j)),
        compiler_params=_cparams("parallel", "parallel"),
    )(*args)


def _rope_tables(t, d):
    pos = jnp.arange(t, dtype=F32)
    inv = ROPE_THETA ** (-jnp.arange(0, d, 2, dtype=F32) / d)
    ang = pos[:, None] * inv[None, :]
    cos, sin = jnp.cos(ang), jnp.sin(ang)
    reps = LANES // d
    return (jnp.tile(jnp.concatenate([cos, cos], -1), (1, reps)),
            jnp.tile(jnp.concatenate([-sin, sin], -1), (1, reps)))


def _rope64(y, cos, sin, lane):
    partner = jnp.where((lane & 32) == 0, pltpu.roll(y, 96, 1), pltpu.roll(y, 32, 1))
    return y * cos + partner * sin


def _rope128(y, cos, sin):
    return y * cos + pltpu.roll(y, 64, 1) * sin


def _seg64_mean_sq(x, lane):
    x2 = x * x
    lo = jnp.sum(jnp.where(lane < 64, x2, 0.0), axis=-1, keepdims=True)
    hi = jnp.sum(jnp.where(lane >= 64, x2, 0.0), axis=-1, keepdims=True)
    return jnp.where(lane < 64, lo, hi) * (1.0 / 64)


def _da_prep_kernel(q_ref, k_ref, v_ref, cos_ref, sin_ref, qg_ref, kg_ref, qo_ref, ko_ref, vo_ref):
    lane = lax.broadcasted_iota(jnp.int32, (1, LANES), 1)
    cos, sin = cos_ref[...], sin_ref[...]
    for c in range(DA_WIDTH // LANES):
        sl = slice(c * LANES, (c + 1) * LANES)
        for x_ref, g_ref, o_ref, scale in ((q_ref, qg_ref, qo_ref, DA_DIM ** -0.5 * LOG2E), (k_ref, kg_ref, ko_ref, 1.0)):
            x = x_ref[:, sl]
            y = x * lax.rsqrt(_seg64_mean_sq(x, lane) + NORM_EPS) * g_ref[...]
            y = _rope64(y, cos, sin, lane)
            o_ref[:, sl] = (y * scale).astype(BF16)
    vo_ref[...] = v_ref[...].T.astype(BF16)


def da_prep(proj, colblk, cos, sin, qg, kg):
    t = proj.shape[0]
    tm = _tile(t, 256)
    w = DA_WIDTH
    col = lambda c: pl.BlockSpec((tm, w), lambda i, c=c: (i, c))
    row = pl.BlockSpec((tm, LANES), lambda i: (i, 0))
    vec = pl.BlockSpec((1, LANES), lambda i: (0, 0))
    out = pl.BlockSpec((tm, w), lambda i: (i, 0))
    shp = jax.ShapeDtypeStruct((t, w), BF16)
    return pl.pallas_call(
        _da_prep_kernel,
        out_shape=(shp, shp, jax.ShapeDtypeStruct((w, t), BF16)),
        grid=(t // tm,),
        in_specs=[col(colblk), col(colblk + 1), col(colblk + 2), row, row, vec, vec],
        out_specs=(out, out, pl.BlockSpec((w, tm), lambda i: (0, i))),
        compiler_params=_cparams("parallel"),
    )(proj, proj, proj, cos, sin, jnp.tile(qg, 2).reshape(1, LANES), jnp.tile(kg, 2).reshape(1, LANES))


ATTN_TILE = 1024


def _causal_steps(t, tile):
    n = t // tile
    pairs = [(i, j) for i in range(n) for j in range(i + 1)]
    return (jnp.asarray([a for a, _ in pairs], jnp.int32), jnp.asarray([b for _, b in pairs], jnp.int32))


SUM_ROWS = 16


def _with_sum_rows(vt):
    return jnp.concatenate([vt, jnp.ones((SUM_ROWS, vt.shape[1]), vt.dtype)], axis=0)


def _softmax_step_t(st, vt1, idx, m_sc, acc_sc):
    m_prev = m_sc[idx]
    m_new = jnp.maximum(m_prev, jnp.max(st, axis=0, keepdims=True))
    p = jnp.exp2(st - m_new).astype(BF16)
    acc_sc[idx] = jnp.exp2(m_prev - m_new) * acc_sc[idx] + jnp.dot(vt1, p, preferred_element_type=F32)
    m_sc[idx] = m_new


def _softmax_result_t(acc):
    return acc[:LANES] / acc[LANES:LANES + 1]


def _init_softmax_stats(m_sc, acc_sc):
    m_sc[...] = jnp.full(m_sc.shape, NEG_BIG, F32)
    acc_sc[...] = jnp.zeros(acc_sc.shape, F32)


def _da_flash_kernel(lam_init, qi_of, kj_of, q_ref, k_ref, vt_ref, lp_ref, g_ref, o_ref, m_sc, acc_sc):
    step = pl.program_id(0)
    qi, kj = qi_of[step], kj_of[step]
    tq, tk = q_ref.shape[0], k_ref.shape[0]

    @pl.when(kj == 0)
    def _():
        _init_softmax_stats(m_sc, acc_sc)

    lane = lax.broadcasted_iota(jnp.int32, (1, LANES), 1)

    def all_heads(masked):
        def one_head(h, carry):
            c = pl.multiple_of(h * LANES, LANES)
            q, k, vt1 = q_ref[:, pl.ds(c, LANES)], k_ref[:, pl.ds(c, LANES)], _with_sum_rows(vt_ref[pl.ds(c, LANES), :])
            for mp in range(2):
                qm = jnp.where((lane < 64) == (mp == 0), q, jnp.zeros_like(q))
                st = lax.dot_general(k, qm, _NT, preferred_element_type=F32)
                if masked:
                    keep = (lax.broadcasted_iota(jnp.int32, (tk, tq), 0) <= lax.broadcasted_iota(jnp.int32, (tk, tq), 1))
                    st = jnp.where(keep, st, NEG_BIG)
                _softmax_step_t(st, vt1, 2 * h + mp, m_sc, acc_sc)
            return carry

        lax.fori_loop(0, DA_HEADS, one_head, 0)

    @pl.when(kj < qi)
    def _():
        all_heads(False)

    @pl.when(kj == qi)
    def _():
        all_heads(True)
        lp = lp_ref[...]
        lam = (jnp.exp(jnp.sum(lp[0:1] * lp[1:2], keepdims=True))
               - jnp.exp(jnp.sum(lp[2:3] * lp[3:4], keepdims=True)) + lam_init)

        def finish(h, carry):
            o = _softmax_result_t(acc_sc[2 * h]) - lam * _softmax_result_t(acc_sc[2 * h + 1])
            y = o * lax.rsqrt(jnp.mean(o * o, axis=0, keepdims=True) + NORM_EPS) * g_ref[...]
            o_ref[:, pl.ds(pl.multiple_of(h * LANES, LANES), LANES)] = (y * (1.0 - lam_init)).T.astype(o_ref.dtype)
            return carry

        lax.fori_loop(0, DA_HEADS, finish, 0)


def da_flash(q, k, vt, lam_params, subln_g, lam_init):
    t = q.shape[0]
    tile = _tile(t, ATTN_TILE)
    qi_of, kj_of = _causal_steps(t, tile)
    grid_spec = pltpu.PrefetchScalarGridSpec(
        num_scalar_prefetch=2,
        grid=(qi_of.shape[0],),
        in_specs=[pl.BlockSpec((tile, DA_WIDTH), lambda s, qi, kj: (qi[s], 0)),
                  pl.BlockSpec((tile, DA_WIDTH), lambda s, qi, kj: (kj[s], 0)),
                  pl.BlockSpec((DA_WIDTH, tile), lambda s, qi, kj: (0, kj[s])),
                  pl.BlockSpec((4, DA_DIM), lambda s, qi, kj: (0, 0)),
                  pl.BlockSpec((LANES, 1), lambda s, qi, kj: (0, 0))],
        out_specs=pl.BlockSpec((tile, DA_WIDTH), lambda s, qi, kj: (qi[s], 0)),
        scratch_shapes=[pltpu.VMEM((2 * DA_HEADS, 1, tile), F32),
                        pltpu.VMEM((2 * DA_HEADS, LANES + SUM_ROWS, tile), F32)])
    return pl.pallas_call(
        functools.partial(_da_flash_kernel, lam_init),
        out_shape=jax.ShapeDtypeStruct((t, DA_WIDTH), BF16),
        grid_spec=grid_spec,
        compiler_params=_cparams("arbitrary"),
    )(qi_of, kj_of, q, k, vt, lam_params, subln_g.reshape(LANES, 1))


GDN_GROUP = 256


def _softplus(x):
    return jnp.maximum(x, 0.0) + jnp.log1p(jnp.exp(-jnp.abs(x)))


def _gdn_gates_kernel(b_ref, a_ref, at_ref, alr_ref, dtr_ref, alc_ref, dtc_ref,
                      beta_ref, gcum_ref, grev_ref, gcumt_ref, egt_ref):
    n = b_ref.shape[0]
    r = lax.broadcasted_iota(jnp.int32, (n, n), 0)
    c = lax.broadcasted_iota(jnp.int32, (n, n), 1)
    same = (r // GDN_CHUNK) == (c // GDN_CHUNK)
    incl = jnp.where(same & (c <= r), 1.0, 0.0)
    rev = jnp.where(same & (c > r), 1.0, 0.0)
    beta_ref[...] = jax.nn.sigmoid(b_ref[...])
    g = -jnp.exp(alr_ref[...]) * _softplus(a_ref[...] + dtr_ref[...])
    gcum_ref[...] = jnp.dot(incl, g, precision=HIGHEST, preferred_element_type=F32)
    grev_ref[...] = jnp.dot(rev, g, precision=HIGHEST, preferred_element_type=F32)
    gt = -jnp.exp(alc_ref[...]) * _softplus(at_ref[...] + dtc_ref[...])
    gcumt = lax.dot_general(gt, incl, _NT, precision=HIGHEST, preferred_element_type=F32)
    gcumt_ref[...] = gcumt
    egt_ref[...] = jnp.exp(gcumt)


def gdn_gates(b_in, a_in, a_t, a_log, dt_bias):
    t, h = b_in.shape
    n = _tile(t, GDN_GROUP)
    colspec = pl.BlockSpec((n, h), lambda i: (i, 0))
    rowspec = pl.BlockSpec((h, n), lambda i: (0, i))
    vr = pl.BlockSpec((1, h), lambda i: (0, 0))
    vc = pl.BlockSpec((h, 1), lambda i: (0, 0))
    cs = jax.ShapeDtypeStruct((t, h), F32)
    rs = jax.ShapeDtypeStruct((h, t), F32)
    return pl.pallas_call(
        _gdn_gates_kernel,
        out_shape=(cs, cs, cs, rs, rs),
        grid=(t // n,),
        in_specs=[colspec, colspec, rowspec, vr, vr, vc, vc],
        out_specs=(colspec, colspec, colspec, rowspec, rowspec),
        compiler_params=_cparams("parallel"),
    )(b_in, a_in, a_t, a_log.reshape(1, h), dt_bias.reshape(1, h), a_log.reshape(h, 1), dt_bias.reshape(h, 1))


def _gdn_local_kernel(xq_ref, xk_ref, xv_ref, hq_ref, hk_ref, hv_ref, wq_ref, wk_ref, wv_ref,
                      beta_ref, gcum_ref, grev_ref, gcumt_ref,
                      u_ref, w_ref, qd_ref, kdt_ref, attn_ref):
    i, hg = pl.program_id(0), pl.program_id(1)
    n = xq_ref.shape[0]
    nh = xq_ref.shape[1] // LANES

    def conv_silu(x_ref, halo_ref, cw_ref, hl):
        x, cw = x_ref[:, hl], cw_ref[:, hl]
        halo = jnp.where(i > 0, halo_ref[:, hl], 0.0)

        def taps(z):
            return (cw[3:4] * z + cw[2:3] * pltpu.roll(z, 1, 0) + cw[1:2] * pltpu.roll(z, 2, 0)
                    + cw[0:1] * pltpu.roll(z, 3, 0))

        top = taps(jnp.concatenate([halo, x[:8]], axis=0))[8:16]
        y = jnp.concatenate([top, taps(x)[8:]], axis=0)
        return y * jax.nn.sigmoid(y)

    def l2n(y):
        return y * lax.rsqrt(jnp.sum(y * y, axis=-1, keepdims=True) + NORM_EPS)

    r = lax.broadcasted_iota(jnp.int32, (n, n), 0)
    c = lax.broadcasted_iota(jnp.int32, (n, n), 1)
    same = (r // GDN_CHUNK) == (c // GDN_CHUNK)
    heads = range(nh)
    hls = [slice(hb * LANES, (hb + 1) * LANES) for hb in heads]
    pws, tinvs, rest = [], [], []
    for hb in heads:
        hl, h = hls[hb], hg * nh + hb
        q = l2n(conv_silu(xq_ref, hq_ref, wq_ref, hl)) * (GDN_DK ** -0.5)
        k = l2n(conv_silu(xk_ref, hk_ref, wk_ref, hl))
        v = conv_silu(xv_ref, hv_ref, wv_ref, hl)
        head = lax.broadcasted_iota(jnp.int32, (1, GDN_HEADS), 1) == h
        pick = lambda ref: jnp.sum(jnp.where(head, ref[...], 0.0), axis=-1, keepdims=True)
        beta, gc, grev = pick(beta_ref), pick(gcum_ref), pick(grev_ref)
        gr = gcumt_ref[pl.ds(h, 1), :]
        decay = jnp.exp(jnp.where(same & (c <= r), gc - gr, NEG_BIG))
        kb = k * beta
        k16 = k.astype(BF16)
        kk = lax.dot_general(kb.astype(BF16), k16, _NT, preferred_element_type=F32)
        qk = lax.dot_general(q.astype(BF16), k16, _NT, preferred_element_type=F32)
        pw = -jnp.where(same & (c < r), kk * decay, 0.0)
        pws.append(pw)
        tinvs.append(jnp.where(r == c, 1.0, 0.0) + pw)
        attn = qk * decay
        qd_ref[:, hl] = (q * jnp.exp(gc)).astype(BF16)
        kdt_ref[hl, :] = (k * jnp.exp(grev)).T.astype(BF16)
        left = attn[:, :LANES]
        for j in range(1, n // LANES):
            left = left + attn[:, j * LANES:(j + 1) * LANES]
        attn_ref[:, hl] = (left + pltpu.roll(left, 64, 1)).astype(BF16)
        rest.append(((v * beta).astype(BF16), (kb * jnp.exp(gc)).astype(BF16)))

    for _ in range(5):
        for hb in heads:
            p16 = pws[hb].astype(BF16)
            pws[hb] = jnp.dot(p16, p16, preferred_element_type=F32)
            tinvs[hb] = tinvs[hb] + jnp.dot(tinvs[hb].astype(BF16), pws[hb].astype(BF16), preferred_element_type=F32)
    for hb in heads:
        t16 = tinvs[hb].astype(BF16)
        u_ref[:, hls[hb]] = jnp.dot(t16, rest[hb][0], preferred_element_type=F32)
        w_ref[:, hls[hb]] = jnp.dot(t16, rest[hb][1], preferred_element_type=F32).astype(BF16)


GDN_LOCAL_HEADS = 4


def gdn_local(proj, qblk, conv_w, beta, gcum, grev, gcumt):
    t = proj.shape[0]
    n = _tile(t, GDN_GROUP)
    nh, hl = GDN_HEADS, GDN_LOCAL_HEADS
    wide = hl * LANES
    assert qblk % hl == 0 and nh % hl == 0
    x = lambda part: pl.BlockSpec((n, wide), lambda i, h, part=part: (i, (qblk + part * nh) // hl + h))
    halo = lambda part: pl.BlockSpec(
        (8, wide), lambda i, h, part=part: (jnp.maximum(i * (n // 8) - 1, 0), (qblk + part * nh) // hl + h))
    cw = lambda part: pl.BlockSpec((GDN_CONV, wide), lambda i, h, part=part: (0, part * nh // hl + h))
    col = pl.BlockSpec((n, nh), lambda i, h: (i, 0))
    row = pl.BlockSpec((nh, n), lambda i, h: (0, i))
    out = pl.BlockSpec((n, wide), lambda i, h: (i, h))
    f32o = jax.ShapeDtypeStruct((t, GDN_WIDTH), F32)
    b16o = jax.ShapeDtypeStruct((t, GDN_WIDTH), BF16)
    return pl.pallas_call(
        _gdn_local_kernel,
        out_shape=(f32o, b16o, b16o, jax.ShapeDtypeStruct((GDN_WIDTH, t), BF16), b16o),
        grid=(t // n, nh // hl),
        in_specs=[x(0), x(1), x(2), halo(0), halo(1), halo(2), cw(0), cw(1), cw(2), col, col, col, row],
        out_specs=(out, out, out, pl.BlockSpec((wide, n), lambda i, h: (h, i)), out),
        compiler_params=_cparams("parallel", "parallel"),
    )(proj, proj, proj, proj, proj, proj, conv_w, conv_w, conv_w, beta, gcum, grev, gcumt)


GDN_SCAN_HEADS = 4


def _gdn_scan_kernel(egl_ref, u_ref, w_ref, qd_ref, kdt_ref, attn_ref, z_ref, g_ref, o_ref, s_ref):
    hg, i = pl.program_id(0), pl.program_id(1)
    n = u_ref.shape[0]
    cpg = n // GDN_CHUNK
    nh = s_ref.shape[0]

    @pl.when(i == 0)
    def _():
        s_ref[...] = jnp.zeros(s_ref.shape, F32)

    states = [s_ref[hb] for hb in range(nh)]
    zeros = jnp.zeros((GDN_CHUNK, GDN_DV), BF16)
    for c in range(cpg):
        sl = slice(c * GDN_CHUNK, (c + 1) * GDN_CHUNK)
        pair = slice((c // 2) * LANES, (c // 2 + 1) * LANES)
        for hb in range(nh):
            hl = slice(hb * LANES, (hb + 1) * LANES)
            s16 = states[hb].astype(BF16)
            vnew = u_ref[sl, hl] - jnp.dot(w_ref[sl, hl], s16, preferred_element_type=F32)
            v16 = vnew.astype(BF16)
            o = (jnp.dot(qd_ref[sl, hl], s16, preferred_element_type=F32)
                 + jnp.dot(attn_ref[sl, hb * LANES:hb * LANES + GDN_CHUNK], v16, preferred_element_type=F32))
            vpad = jnp.concatenate([v16, zeros] if c % 2 == 0 else [zeros, v16], axis=0)
            states[hb] = (states[hb] * egl_ref[hg * nh + hb, i * cpg + c]
                          + jnp.dot(kdt_ref[hl, pair], vpad, preferred_element_type=F32))
            y = o * lax.rsqrt(jnp.mean(o * o, axis=-1, keepdims=True) + NORM_EPS) * g_ref[...]
            z = z_ref[sl, hl]
            o_ref[sl, hl] = (y * (z * jax.nn.sigmoid(z))).astype(o_ref.dtype)
    for hb in range(nh):
        s_ref[hb] = states[hb]


def gdn_scan(egl, u, w, qd, kdt, attn, proj, zblk, norm_g):
    t = u.shape[0]
    n = _tile(t, GDN_GROUP)
    nh = GDN_SCAN_HEADS
    wide = nh * LANES
    assert zblk % nh == 0 and GDN_HEADS % nh == 0
    blk = pl.BlockSpec((n, wide), lambda h, i: (i, h))
    return pl.pallas_call(
        _gdn_scan_kernel,
        out_shape=jax.ShapeDtypeStruct((t, GDN_WIDTH), BF16),
        grid=(GDN_HEADS // nh, t // n),
        in_specs=[pl.BlockSpec(memory_space=pltpu.SMEM), blk, blk, blk,
                  pl.BlockSpec((wide, n), lambda h, i: (h, i)), blk,
                  pl.BlockSpec((n, wide), lambda h, i: (i, zblk // nh + h)),
                  pl.BlockSpec((1, LANES), lambda h, i: (0, 0))],
        out_specs=blk,
        scratch_shapes=[pltpu.VMEM((nh, GDN_DK, GDN_DV), F32)],
        compiler_params=_cparams("parallel", "arbitrary"),
    )(egl, u, w, qd, kdt, attn, proj, norm_g.reshape(1, LANES))


def gated_deltanet(proj, small_b, small_a, qblk, zblk, conv_w, a_log, dt_bias, norm_g):
    beta, gcum, grev, gcumt, egt = gdn_gates(small_b, small_a, small_a.T, a_log, dt_bias)
    u, w, qd, kdt, attn = gdn_local(proj, qblk, conv_w, beta, gcum, grev, gcumt)
    egl = egt[:, GDN_CHUNK - 1::GDN_CHUNK]
    return gdn_scan(egl, u, w, qd, kdt, attn, proj, zblk, norm_g)


DSA_SEL_TQ = 128
DSA_SEL_TK = 256


def _dsa_prep_kernel(q_ref, k_ref, v_ref, qi_ref, ki_ref, wi_ref, cos64_ref, sin64_ref, cos128_ref, sin128_ref,
                     qg_ref, kg_ref, qo_ref, ko_ref, vo_ref, qio_ref, kio_ref, wio_ref):
    lane = lax.broadcasted_iota(jnp.int32, (1, LANES), 1)
    cos64, sin64 = cos64_ref[...], sin64_ref[...]
    cos128, sin128 = cos128_ref[...], sin128_ref[...]
    for c in range(DSA_HEADS):
        sl = slice(c * LANES, (c + 1) * LANES)
        for x_ref, g_ref, o_ref, scale in ((q_ref, qg_ref, qo_ref, DSA_DIM ** -0.5 * LOG2E), (k_ref, kg_ref, ko_ref, 1.0)):
            x = x_ref[:, sl]
            y = x * lax.rsqrt(jnp.mean(x * x, axis=-1, keepdims=True) + NORM_EPS) * g_ref[...]
            o_ref[:, sl] = (_rope128(y, cos128, sin128) * scale).astype(BF16)
    vo_ref[...] = v_ref[...].T.astype(BF16)
    tq = DSA_SEL_TQ
    for c in range(IDX_HEADS * IDX_DIM // LANES):
        y = _rope64(qi_ref[:, c * LANES:(c + 1) * LANES], cos64, sin64, lane).astype(BF16)
        for b in range(qi_ref.shape[0] // tq):
            for half in range(2):
                h = 2 * c + half
                qio_ref[b, h * tq:(h + 1) * tq, :] = y[b * tq:(b + 1) * tq, half * IDX_DIM:(half + 1) * IDX_DIM]
    kio_ref[...] = _rope64(ki_ref[...], cos64, sin64, lane).astype(BF16)
    wio_ref[...] = wi_ref[...] * ((IDX_HEADS * IDX_DIM) ** -0.5)


def dsa_prep(proj, qblk, qiblk, ki, wi, tabs64, tabs128, qg, kg):
    t = proj.shape[0]
    tm = _tile(t, 256)
    w = DSA_WIDTH
    nb = tm // DSA_SEL_TQ
    col = lambda c: pl.BlockSpec((tm, w), lambda i, c=c: (i, c))
    row = pl.BlockSpec((tm, LANES), lambda i: (i, 0))
    vec = pl.BlockSpec((1, LANES), lambda i: (0, 0))
    out = pl.BlockSpec((tm, w), lambda i: (i, 0))
    shp = jax.ShapeDtypeStruct((t, w), BF16)
    return pl.pallas_call(
        _dsa_prep_kernel,
        out_shape=(shp, shp, jax.ShapeDtypeStruct((w, t), BF16),
                   jax.ShapeDtypeStruct((t // DSA_SEL_TQ, IDX_HEADS * DSA_SEL_TQ, IDX_DIM), BF16),
                   jax.ShapeDtypeStruct((t, LANES), BF16), jax.ShapeDtypeStruct((t, IDX_HEADS), F32)),
        grid=(t // tm,),
        in_specs=[col(qblk), col(qblk + 1), col(qblk + 2),
                  pl.BlockSpec((tm, IDX_HEADS * IDX_DIM), lambda i: (i, qiblk)), row,
                  pl.BlockSpec((tm, IDX_HEADS), lambda i: (i, 0)), row, row, row, row, vec, vec],
        out_specs=(out, out, pl.BlockSpec((w, tm), lambda i: (0, i)),
                   pl.BlockSpec((nb, IDX_HEADS * DSA_SEL_TQ, IDX_DIM), lambda i: (i, 0, 0)), row,
                   pl.BlockSpec((tm, IDX_HEADS), lambda i: (i, 0))),
        compiler_params=_cparams("parallel"),
    )(proj, proj, proj, proj, ki, wi, *tabs64, *tabs128, qg.reshape(1, LANES), kg.reshape(1, LANES))


def _order_key(x):
    bits = pltpu.bitcast(x, jnp.int32)
    return jnp.where(bits < 0, bits ^ 0x7FFFFFFF, bits)


def _order_value(key):
    return pltpu.bitcast(jnp.where(key < 0, key ^ 0x7FFFFFFF, key), F32)


def _dsa_select_kernel(n_sel, qi_ref, ki_ref, wt_ref, mask_ref, keys_sc):
    i = pl.program_id(0)
    tq, tk = DSA_SEL_TQ, DSA_SEL_TK
    t = ki_ref.shape[0]
    nvalid = ((i + 1) * tq + tk - 1) // tk
    qpos = i * tq + lax.broadcasted_iota(jnp.int32, (1, tq), 1)
    w = wt_ref[...]
    q_all = qi_ref[0]
    chunk = lambda c: pl.ds(pl.multiple_of(c * tk, tk), tk)

    def score_chunk(c, carry):
        lg = lax.dot_general(ki_ref[chunk(c), :IDX_DIM], q_all, _NT, preferred_element_type=F32)
        acc = jnp.zeros((tk, tq), F32)
        for h in range(IDX_HEADS):
            acc = acc + w[h:h + 1, :] * jnp.maximum(lg[:, h * tq:(h + 1) * tq], 0.0)
        key = _order_key(acc)
        kpos = c * tk + lax.broadcasted_iota(jnp.int32, (tk, 1), 0)
        causal = kpos <= qpos
        keys_sc[chunk(c), :] = jnp.where(causal, key, INT_MIN)
        kmin8, kmax8, sum8, sq8 = carry
        fold = lambda x: x.reshape(tk // 8, 8, tq)
        live = jnp.where(causal, acc, 0.0)
        return (jnp.minimum(kmin8, jnp.min(fold(jnp.where(causal, key, INT_MAX)), axis=0)),
                jnp.maximum(kmax8, jnp.max(fold(jnp.where(causal, key, INT_MIN)), axis=0)),
                sum8 + jnp.sum(fold(live), axis=0), sq8 + jnp.sum(fold(live * live), axis=0))

    zeros8 = jnp.zeros((8, tq), F32)
    kmin8, kmax8, sum8, sq8 = lax.fori_loop(0, nvalid, score_chunk, (
        jnp.full((8, tq), INT_MAX, jnp.int32), jnp.full((8, tq), INT_MIN, jnp.int32), zeros8, zeros8))

    def count_ge(thr):
        def body(c, acc):
            hit = jnp.where(keys_sc[chunk(c), :] >= thr, 1, 0)
            return acc + jnp.sum(hit.reshape(tk // 8, 8, tq), axis=0)

        acc = lax.fori_loop(0, nvalid, body, jnp.zeros((8, tq), jnp.int32))
        return jnp.sum(acc, axis=0, keepdims=True)

    lo0 = jnp.min(kmin8, axis=0, keepdims=True)
    hi0 = jnp.max(kmax8, axis=0, keepdims=True) + 1
    cnt_lo0 = qpos + 1
    cnt_hi0 = jnp.zeros((1, tq), jnp.int32)

    def unsettled(lo, hi, cnt_lo):
        return jnp.max(jnp.where((cnt_lo > n_sel) & (lo + 1 != hi), 1, 0))

    n_live = cnt_lo0.astype(F32)
    mean = jnp.sum(sum8, axis=0, keepdims=True) / n_live
    dev = jnp.sqrt(jnp.maximum(jnp.sum(sq8, axis=0, keepdims=True) / n_live - mean * mean, 0.0))
    tail = jnp.clip(n_sel / n_live, 1e-6, 0.5)
    tt = jnp.sqrt(-2.0 * jnp.log(tail))
    z = tt - (2.30753 + 0.27061 * tt) / (1.0 + 0.99229 * tt + 0.04481 * tt * tt)
    slope = dev / (z + 1.0 / jnp.maximum(z, 0.5))
    log_n = math.log(n_sel)

    def narrow(carry):
        lo, hi, cnt_lo, cnt_hi, w_lo, w_hi, last_ok, it, _ = carry
        half = (lo >> 1) + (hi >> 1) + (lo & hi & 1)
        v_lo, v_hi = _order_value(lo), _order_value(hi)
        e_lo, e_hi = jnp.log(cnt_lo.astype(F32) + 0.5) - log_n, jnp.log(cnt_hi.astype(F32) + 0.5) - log_n
        f_lo, f_hi = e_lo * w_lo, e_hi * w_hi
        false_pos = v_lo + f_lo / (f_lo - f_hi) * (v_hi - v_lo)
        along_tail = jnp.where(last_ok > 0, v_lo + e_lo * slope, v_hi + e_hi * slope)
        guess = _order_key(jnp.where(it == 0, mean + z * dev, jnp.where(it == 1, along_tail, false_pos)))
        mid = jnp.where((guess > lo) & (guess < hi) & ((it & 7) != 7), guess, half)
        cnt = count_ge(mid)
        ok = cnt >= n_sel
        w_hi = jnp.where(ok, jnp.where(last_ok > 0, 0.5 * w_hi, w_hi), 1.0)
        w_lo = jnp.where(ok, 1.0, jnp.where(last_ok < 0, 0.5 * w_lo, w_lo))
        lo, cnt_lo = jnp.where(ok, mid, lo), jnp.where(ok, cnt, cnt_lo)
        hi, cnt_hi = jnp.where(ok, hi, mid), jnp.where(ok, cnt_hi, cnt)
        return lo, hi, cnt_lo, cnt_hi, w_lo, w_hi, jnp.where(ok, 1, -1), it + 1, unsettled(lo, hi, cnt_lo)

    ones = jnp.ones((1, tq), F32)
    lo = lax.while_loop(lambda cr: cr[8] > 0, narrow,
                        (lo0, hi0, cnt_lo0, cnt_hi0, ones, ones, jnp.zeros((1, tq), jnp.int32), jnp.int32(0),
                         unsettled(lo0, hi0, cnt_lo0)))[0]

    def write_valid(c, carry):
        mask_ref[chunk(c), :] = jnp.where(keys_sc[chunk(c), :] >= lo, 1.0, 0.0).astype(mask_ref.dtype)
        return carry

    def write_zero(c, carry):
        mask_ref[chunk(c), :] = jnp.zeros((tk, tq), mask_ref.dtype)
        return carry

    lax.fori_loop(0, nvalid, write_valid, 0)
    lax.fori_loop(nvalid, t // tk, write_zero, 0)


def dsa_select(qi, ki, wt, n_sel):
    t = ki.shape[0]
    tq = DSA_SEL_TQ
    return pl.pallas_call(
        functools.partial(_dsa_select_kernel, n_sel),
        out_shape=jax.ShapeDtypeStruct((t, t), BF16),
        grid=(t // tq,),
        in_specs=[pl.BlockSpec((1, IDX_HEADS * tq, IDX_DIM), lambda i: (i, 0, 0)),
                  pl.BlockSpec((t, LANES), lambda i: (0, 0)),
                  pl.BlockSpec((IDX_HEADS, tq), lambda i: (0, i))],
        out_specs=pl.BlockSpec((t, tq), lambda i: (0, i)),
        scratch_shapes=[pltpu.VMEM((t, tq), jnp.int32)],
        compiler_params=_cparams("parallel"),
    )(qi, ki, wt)


def _dsa_flash_kernel(qi_of, kj_of, q_ref, k_ref, vt_ref, mask_ref, o_ref, m_sc, acc_sc):
    step = pl.program_id(0)
    qi, kj = qi_of[step], kj_of[step]

    @pl.when(kj == 0)
    def _():
        _init_softmax_stats(m_sc, acc_sc)

    def one_head(h, carry):
        c = pl.ds(pl.multiple_of(h * LANES, LANES), LANES)
        st = lax.dot_general(k_ref[:, c], q_ref[:, c], _NT, preferred_element_type=F32)
        st = jnp.where(mask_ref[...] > 0, st, NEG_BIG)
        _softmax_step_t(st, _with_sum_rows(vt_ref[c, :]), h, m_sc, acc_sc)
        return carry

    lax.fori_loop(0, DSA_HEADS, one_head, 0)

    @pl.when(kj == qi)
    def _():
        def finish(h, carry):
            o_ref[:, pl.ds(pl.multiple_of(h * LANES, LANES), LANES)] = _softmax_result_t(acc_sc[h]).T.astype(o_ref.dtype)
            return carry

        lax.fori_loop(0, DSA_HEADS, finish, 0)


def dsa_flash(q, k, vt, mask_t):
    t = q.shape[0]
    tile = _tile(t, ATTN_TILE)
    qi_of, kj_of = _causal_steps(t, tile)
    grid_spec = pltpu.PrefetchScalarGridSpec(
        num_scalar_prefetch=2,
        grid=(qi_of.shape[0],),
        in_specs=[pl.BlockSpec((tile, DSA_WIDTH), lambda s, qi, kj: (qi[s], 0)),
                  pl.BlockSpec((tile, DSA_WIDTH), lambda s, qi, kj: (kj[s], 0)),
                  pl.BlockSpec((DSA_WIDTH, tile), lambda s, qi, kj: (0, kj[s])),
                  pl.BlockSpec((tile, tile), lambda s, qi, kj: (kj[s], qi[s]))],
        out_specs=pl.BlockSpec((tile, DSA_WIDTH), lambda s, qi, kj: (qi[s], 0)),
        scratch_shapes=[pltpu.VMEM((DSA_HEADS, 1, tile), F32),
                        pltpu.VMEM((DSA_HEADS, LANES + SUM_ROWS, tile), F32)])
    return pl.pallas_call(
        _dsa_flash_kernel,
        out_shape=jax.ShapeDtypeStruct((t, DSA_WIDTH), BF16),
        grid_spec=grid_spec,
        compiler_params=_cparams("arbitrary"),
    )(qi_of, kj_of, q, k, vt, mask_t)


def dsa_attention(proj, qblk, qiblk, ki, wi, tabs64, tabs128, qg, kg):
    t = proj.shape[0]
    q, k, vt, qi, kir, wis = dsa_prep(proj, qblk, qiblk, ki, wi, tabs64, tabs128, qg, kg)
    mask_t = dsa_select(qi, kir, wis.T, min(TOPK_MAX, t // 4))
    return dsa_flash(q, k, vt, mask_t)


def _take_top(s, count):
    n = s.shape[0]
    idx = lax.broadcasted_iota(jnp.int32, s.shape, 0)
    vals = []
    for _ in range(count):
        m = jnp.max(s, axis=0, keepdims=True)
        first = jnp.min(jnp.where(s == m, idx, n), axis=0, keepdims=True)
        s = jnp.where(idx == first, NEG_BIG, s)
        vals.append(m)
    return vals


def _peer_route_kernel(q_ref, sk_ref, s1_ref, s2_ref, e1_ref, e2_ref, tau_ref):
    for h in range(PEER_HEADS):
        halves = []
        for c in range(2):
            sl = slice((2 * h + c) * LANES, (2 * h + c + 1) * LANES)
            halves.append(lax.dot_general(sk_ref[h, c], q_ref[:, sl].astype(BF16), _NT, preferred_element_type=F32))
        s1, s2 = halves
        v1, v2 = _take_top(s1, PEER_TOPK), _take_top(s2, PEER_TOPK)
        v1_all, v2_all = jnp.concatenate(v1, axis=0), jnp.concatenate(v2, axis=0)
        cand = jnp.concatenate([v1[0] + v2_all] + [v1[a] + v2_all[:8] for a in range(1, 8)] + [v1_all[8:] + v2[0]],
                               axis=0)
        top = _take_top(cand, PEER_TOPK)
        z = sum(jnp.exp(tv - top[0]) for tv in top)
        s1_ref[h], s2_ref[h] = s1, s2
        e1_ref[h] = jnp.exp(s1 - v1[0])
        e2_ref[h] = jnp.exp(s2 - v2[0]) / z
        tau_ref[h:h + 1, :] = top[-1]


def peer_route(q, sub_keys):
    t = q.shape[0]
    tm = _tile(t, 256)
    big = pl.BlockSpec((PEER_HEADS, PEER_NKEYS, tm), lambda i: (0, 0, i))
    bs = jax.ShapeDtypeStruct((PEER_HEADS, PEER_NKEYS, t), F32)
    return pl.pallas_call(
        _peer_route_kernel,
        out_shape=(bs, bs, bs, bs, jax.ShapeDtypeStruct((PEER_HEADS, t), F32)),
        grid=(t // tm,),
        in_specs=[pl.BlockSpec((tm, PEER_HEADS * PEER_DKEY), lambda i: (i, 0)),
                  pl.BlockSpec(sub_keys.shape, lambda i: (0, 0, 0, 0))],
        out_specs=(big, big, big, big, pl.BlockSpec((PEER_HEADS, tm), lambda i: (0, i))),
        compiler_params=_cparams("parallel"),
    )(q, sub_keys)


def _gelu_tanh(x):
    return 0.5 * x * (1.0 + jnp.tanh(math.sqrt(2.0 / math.pi) * (x + 0.044715 * (x * x * x))))


def _peer_expert_kernel(x_ref, u_ref, vt_ref, s1_ref, s2_ref, e1_ref, e2_ref, tau_ref, o_ref):
    j = pl.program_id(1)
    te = u_ref.shape[0]

    @pl.when(j == 0)
    def _():
        o_ref[...] = jnp.zeros(o_ref.shape, F32)

    act = _gelu_tanh(lax.dot_general(u_ref[...], x_ref[...], _NT, preferred_element_type=F32))
    rows_per = te // PEER_NKEYS
    gates = []
    for a in range(rows_per):
        n1 = j * rows_per + a
        g = None
        for h in range(PEER_HEADS):
            pair = s1_ref[h, pl.ds(n1, 1), :] + s2_ref[h]
            gh = jnp.where(pair >= tau_ref[h:h + 1, :], e1_ref[h, pl.ds(n1, 1), :] * e2_ref[h], 0.0)
            g = gh if g is None else g + gh
        gates.append(g)
    ga = (jnp.concatenate(gates, axis=0) * act).astype(BF16)
    o_ref[...] += jnp.dot(vt_ref[...], ga, preferred_element_type=F32)


def peer_expert(x, u, vt, s1, s2, e1, e2, tau, tm=512, te=512):
    t, d = x.shape
    e = u.shape[0]
    tm, te = _tile(t, tm), _tile(e, te)
    once = pl.Buffered(1)
    big = pl.BlockSpec((PEER_HEADS, PEER_NKEYS, tm), lambda i, j: (0, 0, i), pipeline_mode=once)
    return pl.pallas_call(
        _peer_expert_kernel,
        out_shape=jax.ShapeDtypeStruct((d, t), F32),
        grid=(t // tm, e // te),
        in_specs=[pl.BlockSpec((tm, d), lambda i, j: (i, 0), pipeline_mode=once),
                  pl.BlockSpec((te, d), lambda i, j: (j, 0)),
                  pl.BlockSpec((d, te), lambda i, j: (0, j)),
                  big, big, big, big,
                  pl.BlockSpec((PEER_HEADS, tm), lambda i, j: (0, i), pipeline_mode=once)],
        out_specs=pl.BlockSpec((d, tm), lambda i, j: (0, i)),
        compiler_params=_cparams("parallel", "arbitrary"),
    )(x, u, vt, s1, s2, e1, e2, tau)


def _add_t_rmsnorm_kernel(h_ref, pt_ref, g_ref, ho_ref, ao_ref):
    x = h_ref[...] + pt_ref[...].T
    ho_ref[...] = x
    ao_ref[...] = (x * lax.rsqrt(jnp.mean(x * x, axis=-1, keepdims=True) + NORM_EPS) * g_ref[...]).astype(BF16)


def add_t_rmsnorm(h, pt, g):
    t, d = h.shape
    tm = _tile(t, 256)
    blk = pl.BlockSpec((tm, d), lambda i: (i, 0))
    return pl.pallas_call(
        _add_t_rmsnorm_kernel,
        out_shape=(jax.ShapeDtypeStruct((t, d), F32), jax.ShapeDtypeStruct((t, d), BF16)),
        grid=(t // tm,),
        in_specs=[blk, pl.BlockSpec((d, tm), lambda i: (0, i)), pl.BlockSpec((1, d), lambda i: (0, 0))],
        out_specs=(blk, blk),
        compiler_params=_cparams("parallel"),
    )(h, pt, g.reshape(1, d))


def _ple_kernel(a_ref, wg_ref, p_ref, wp_ref, h_ref, o_ref):
    gate = jax.nn.sigmoid(jnp.dot(a_ref[...], wg_ref[...], preferred_element_type=F32))
    o_ref[...] = h_ref[...] + gate * jnp.dot(p_ref[...], wp_ref[...], preferred_element_type=F32)


def ple(a, wg, p, wp, h, tm=512, tn=1024):
    m, k = a.shape
    n = wg.shape[1]
    kp = p.shape[1]
    tm, tn = _tile(m, tm), _tile(n, tn)
    return pl.pallas_call(
        _ple_kernel,
        out_shape=jax.ShapeDtypeStruct((m, n), F32),
        grid=(n // tn, m // tm),
        in_specs=[pl.BlockSpec((tm, k), lambda j, i: (i, 0)), pl.BlockSpec((k, tn), lambda j, i: (0, j)),
                  pl.BlockSpec((tm, kp), lambda j, i: (i, 0)), pl.BlockSpec((kp, tn), lambda j, i: (0, j)),
                  pl.BlockSpec((tm, tn), lambda j, i: (i, j))],
        out_specs=pl.BlockSpec((tm, tn), lambda j, i: (i, j)),
        compiler_params=_cparams("parallel", "parallel"),
    )(a, wg, p, wp, h)


_QBLK_GDN, _ZBLK_GDN = 0, 48
_BLK_DA = 8
_BLK_DSA = 11
_BLK_IDX = 7


def _regroup_w_in(w):
    sizes = (DA_WIDTH, DA_WIDTH, DA_WIDTH, 3 * GDN_WIDTH, GDN_WIDTH, GDN_HEADS, GDN_HEADS,
             DSA_WIDTH, DSA_WIDTH, DSA_WIDTH, IDX_HEADS * IDX_DIM, IDX_DIM, IDX_HEADS)
    offs = [0]
    for s in sizes:
        offs.append(offs[-1] + s)
    (da_q, da_k, da_v, g_qkv, g_z, g_b, g_a, c_q, c_k, c_v, c_qi, c_ki, c_w) = [
        w[:, offs[n]:offs[n + 1]] for n in range(len(sizes))]
    main = jnp.concatenate([g_qkv, g_z, da_q, da_k, da_v, c_q, c_k, c_v, c_qi], axis=1).astype(BF16)
    small = jnp.concatenate([c_ki, c_w, g_b, g_a], axis=1).astype(BF16)
    return main, small


def kernel(x, p, attn_norm, w_in, da_q_norm, da_k_norm, da_lambda, da_subln, gdn_conv, gdn_a_log, gdn_dt_bias, gdn_norm, dsa_q_norm, dsa_k_norm, w_out, ffn_norm, peer_w_q, peer_sub_keys, peer_u, peer_v, ple_norm, w_ple_gate, w_ple_proj):
    b, t, d = x.shape
    assert b == 1
    depth = w_in.shape[0]
    tabs64, tabs128 = _rope_tables(t, 64), _rope_tables(t, 128)
    h = x.reshape(t, d)
    for i in range(depth):
        w_main, w_small = _regroup_w_in(w_in[i])
        a = rmsnorm_bf16(h, attn_norm[i])
        proj = matmul(a, w_main)
        small = matmul(a, w_small)
        lam_init = 0.8 - 0.6 * math.exp(-0.3 * i)
        qa, ka, va = da_prep(proj, _BLK_DA, *tabs64, da_q_norm[i], da_k_norm[i])
        o_a = da_flash(qa, ka, va, da_lambda[i], da_subln[i], lam_init)
        o_b = gated_deltanet(proj, small[:, 96:112], small[:, 112:128], _QBLK_GDN, _ZBLK_GDN, gdn_conv[i],
                             gdn_a_log[i], gdn_dt_bias[i], gdn_norm[i])
        o_c = dsa_attention(proj, _BLK_DSA, _BLK_IDX, small, small[:, 64:96], tabs64, tabs128,
                            dsa_q_norm[i], dsa_k_norm[i])
        mix = jnp.concatenate([o_a, o_b, o_c], axis=-1)
        h = matmul(mix, w_out[i].astype(BF16), residual=h)

        a = rmsnorm_bf16(h, ffn_norm[i])
        pq = matmul(a, peer_w_q[i].astype(BF16))
        s1, s2, e1, e2, tau = peer_route(pq, peer_sub_keys[i].astype(BF16))
        pt = peer_expert(a, peer_u[i].astype(BF16), peer_v[i].T.astype(BF16), s1, s2, e1, e2, tau)
        h, a = add_t_rmsnorm(h, pt, ple_norm[i])
        h = ple(a, w_ple_gate[i].astype(BF16), p[i, 0].astype(BF16), w_ple_proj[i].astype(BF16), h)
    return h.reshape(b, t, d)
```

```python
import functools
import math

import jax
import jax.numpy as jnp
from jax import lax
from jax.experimental import pallas as pl
from jax.experimental.pallas import tpu as pltpu

F32 = jnp.float32
BF16 = jnp.bfloat16
HIGHEST = lax.Precision.HIGHEST

NORM_EPS = 1e-6
ROPE_THETA = 10000.0
LANES = 128
VMEM_LIMIT = 56 * 1024 * 1024

DA_HEADS, DA_DIM = 8, 64
DA_WIDTH = DA_HEADS * 2 * DA_DIM
GDN_HEADS, GDN_DK, GDN_DV, GDN_CONV, GDN_CHUNK = 16, 128, 128, 4, 64
GDN_WIDTH = GDN_HEADS * GDN_DV
DSA_HEADS, DSA_DIM = 8, 128
DSA_WIDTH = DSA_HEADS * DSA_DIM
IDX_HEADS, IDX_DIM = 32, 64
TOPK_MAX = 256
PEER_HEADS, PEER_NKEYS, PEER_DKEY, PEER_TOPK = 8, 128, 256, 16
PLE_DIM = 256

LOG2E = math.log2(math.e)
NEG_BIG = -1e30
INT_MIN = -(2 ** 31)
INT_MAX = 2 ** 31 - 1

_NT = (((1,), (1,)), ((), ()))


def _cparams(*sem):
    return pltpu.CompilerParams(dimension_semantics=sem, vmem_limit_bytes=VMEM_LIMIT)


def _tile(n, pref):
    t = min(n, pref)
    assert n % t == 0, (n, pref)
    return t


def _rmsnorm_kernel(x_ref, g_ref, o_ref):
    x = x_ref[...]
    y = x * lax.rsqrt(jnp.mean(x * x, axis=-1, keepdims=True) + NORM_EPS) * g_ref[...]
    o_ref[...] = y.astype(o_ref.dtype)


def rmsnorm_bf16(x, g):
    t, d = x.shape
    tm = _tile(t, 256)
    return pl.pallas_call(
        _rmsnorm_kernel,
        out_shape=jax.ShapeDtypeStruct((t, d), BF16),
        grid=(t // tm,),
        in_specs=[pl.BlockSpec((tm, d), lambda i: (i, 0)), pl.BlockSpec((1, d), lambda i: (0, 0))],
        out_specs=pl.BlockSpec((tm, d), lambda i: (i, 0)),
        compiler_params=_cparams("parallel"),
    )(x, g.reshape(1, d))


def _matmul_kernel(a_ref, b_ref, o_ref):
    o_ref[...] = jnp.dot(a_ref[...], b_ref[...], preferred_element_type=F32)


def _matmul_res_kernel(a_ref, b_ref, r_ref, o_ref):
    o_ref[...] = r_ref[...] + jnp.dot(a_ref[...], b_ref[...], preferred_element_type=F32)


def matmul(a, b, residual=None, tm=512, tn=1024):
    m, k = a.shape
    n = b.shape[1]
    tm, tn = _tile(m, tm), _tile(n, tn)
    in_specs = [pl.BlockSpec((tm, k), lambda j, i: (i, 0)), pl.BlockSpec((k, tn), lambda j, i: (0, j))]
    args = [a, b]
    body = _matmul_kernel
    if residual is not None:
        in_specs.append(pl.BlockSpec((tm, tn), lambda j, i: (i, j)))
        args.append(residual)
        body = _matmul_res_kernel
    return pl.pallas_call(
        body,
        out_shape=jax.ShapeDtypeStruct((m, n), F32),
        grid=(n // tn, m // tm),
        in_specs=in_specs,
        out_specs=pl.BlockSpec((tm, tn), lambda j, i: (i, j)),
        compiler_params=_cparams("parallel", "parallel"),
    )(*args)


def _rope_tables(t, d):
    pos = jnp.arange(t, dtype=F32)
    inv = ROPE_THETA ** (-jnp.arange(0, d, 2, dtype=F32) / d)
    ang = pos[:, None] * inv[None, :]
    cos, sin = jnp.cos(ang), jnp.sin(ang)
    reps = LANES // d
    return (jnp.tile(jnp.concatenate([cos, cos], -1), (1, reps)),
            jnp.tile(jnp.concatenate([-sin, sin], -1), (1, reps)))


def _rope64(y, cos, sin, lane):
    partner = jnp.where((lane & 32) == 0, pltpu.roll(y, 96, 1), pltpu.roll(y, 32, 1))
    return y * cos + partner * sin


def _rope128(y, cos, sin):
    return y * cos + pltpu.roll(y, 64, 1) * sin


def _seg64_mean_sq(x, lane):
    x2 = x * x
    lo = jnp.sum(jnp.where(lane < 64, x2, 0.0), axis=-1, keepdims=True)
    hi = jnp.sum(jnp.where(lane >= 64, x2, 0.0), axis=-1, keepdims=True)
    return jnp.where(lane < 64, lo, hi) * (1.0 / 64)


def _da_prep_kernel(q_ref, k_ref, v_ref, cos_ref, sin_ref, qg_ref, kg_ref, qo_ref, ko_ref, vo_ref):
    lane = lax.broadcasted_iota(jnp.int32, (1, LANES), 1)
    cos, sin = cos_ref[...], sin_ref[...]
    for c in range(DA_WIDTH // LANES):
        sl = slice(c * LANES, (c + 1) * LANES)
        for x_ref, g_ref, o_ref, scale in ((q_ref, qg_ref, qo_ref, DA_DIM ** -0.5 * LOG2E), (k_ref, kg_ref, ko_ref, 1.0)):
            x = x_ref[:, sl]
            y = x * lax.rsqrt(_seg64_mean_sq(x, lane) + NORM_EPS) * g_ref[...]
            y = _rope64(y, cos, sin, lane)
            o_ref[:, sl] = (y * scale).astype(BF16)
    vo_ref[...] = v_ref[...].T.astype(BF16)


def da_prep(proj, colblk, cos, sin, qg, kg):
    t = proj.shape[0]
    tm = _tile(t, 256)
    w = DA_WIDTH
    col = lambda c: pl.BlockSpec((tm, w), lambda i, c=c: (i, c))
    row = pl.BlockSpec((tm, LANES), lambda i: (i, 0))
    vec = pl.BlockSpec((1, LANES), lambda i: (0, 0))
    out = pl.BlockSpec((tm, w), lambda i: (i, 0))
    shp = jax.ShapeDtypeStruct((t, w), BF16)
    return pl.pallas_call(
        _da_prep_kernel,
        out_shape=(shp, shp, jax.ShapeDtypeStruct((w, t), BF16)),
        grid=(t // tm,),
        in_specs=[col(colblk), col(colblk + 1), col(colblk + 2), row, row, vec, vec],
        out_specs=(out, out, pl.BlockSpec((w, tm), lambda i: (0, i))),
        compiler_params=_cparams("parallel"),
    )(proj, proj, proj, cos, sin, jnp.tile(qg, 2).reshape(1, LANES), jnp.tile(kg, 2).reshape(1, LANES))


ATTN_TILE = 1024


def _causal_steps(t, tile):
    n = t // tile
    pairs = [(i, j) for i in range(n) for j in range(i + 1)]
    return (jnp.asarray([a for a, _ in pairs], jnp.int32), jnp.asarray([b for _, b in pairs], jnp.int32))


SUM_ROWS = 16


def _with_sum_rows(vt):
    return jnp.concatenate([vt, jnp.ones((SUM_ROWS, vt.shape[1]), vt.dtype)], axis=0)


def _softmax_step_t(st, vt1, idx, m_sc, acc_sc):
    m_prev = m_sc[idx]
    m_new = jnp.maximum(m_prev, jnp.max(st, axis=0, keepdims=True))
    p = jnp.exp2(st - m_new).astype(BF16)
    acc_sc[idx] = jnp.exp2(m_prev - m_new) * acc_sc[idx] + jnp.dot(vt1, p, preferred_element_type=F32)
    m_sc[idx] = m_new


def _softmax_result_t(acc):
    return acc[:LANES] / acc[LANES:LANES + 1]


def _init_softmax_stats(m_sc, acc_sc):
    m_sc[...] = jnp.full(m_sc.shape, NEG_BIG, F32)
    acc_sc[...] = jnp.zeros(acc_sc.shape, F32)


def _da_flash_kernel(lam_init, qi_of, kj_of, q_ref, k_ref, vt_ref, lp_ref, g_ref, o_ref, m_sc, acc_sc):
    step = pl.program_id(0)
    qi, kj = qi_of[step], kj_of[step]
    tq, tk = q_ref.shape[0], k_ref.shape[0]

    @pl.when(kj == 0)
    def _():
        _init_softmax_stats(m_sc, acc_sc)

    lane = lax.broadcasted_iota(jnp.int32, (1, LANES), 1)

    def all_heads(masked):
        def one_head(h, carry):
            c = pl.multiple_of(h * LANES, LANES)
            q, k, vt1 = q_ref[:, pl.ds(c, LANES)], k_ref[:, pl.ds(c, LANES)], _with_sum_rows(vt_ref[pl.ds(c, LANES), :])
            for mp in range(2):
                qm = jnp.where((lane < 64) == (mp == 0), q, jnp.zeros_like(q))
                st = lax.dot_general(k, qm, _NT, preferred_element_type=F32)
                if masked:
                    keep = (lax.broadcasted_iota(jnp.int32, (tk, tq), 0) <= lax.broadcasted_iota(jnp.int32, (tk, tq), 1))
                    st = jnp.where(keep, st, NEG_BIG)
                _softmax_step_t(st, vt1, 2 * h + mp, m_sc, acc_sc)
            return carry

        lax.fori_loop(0, DA_HEADS, one_head, 0)

    @pl.when(kj < qi)
    def _():
        all_heads(False)

    @pl.when(kj == qi)
    def _():
        all_heads(True)
        lp = lp_ref[...]
        lam = (jnp.exp(jnp.sum(lp[0:1] * lp[1:2], keepdims=True))
               - jnp.exp(jnp.sum(lp[2:3] * lp[3:4], keepdims=True)) + lam_init)

        def finish(h, carry):
            o = _softmax_result_t(acc_sc[2 * h]) - lam * _softmax_result_t(acc_sc[2 * h + 1])
            y = o * lax.rsqrt(jnp.mean(o * o, axis=0, keepdims=True) + NORM_EPS) * g_ref[...]
            o_ref[:, pl.ds(pl.multiple_of(h * LANES, LANES), LANES)] = (y * (1.0 - lam_init)).T.astype(o_ref.dtype)
            return carry

        lax.fori_loop(0, DA_HEADS, finish, 0)


def da_flash(q, k, vt, lam_params, subln_g, lam_init):
    t = q.shape[0]
    tile = _tile(t, ATTN_TILE)
    qi_of, kj_of = _causal_steps(t, tile)
    grid_spec = pltpu.PrefetchScalarGridSpec(
        num_scalar_prefetch=2,
        grid=(qi_of.shape[0],),
        in_specs=[pl.BlockSpec((tile, DA_WIDTH), lambda s, qi, kj: (qi[s], 0)),
                  pl.BlockSpec((tile, DA_WIDTH), lambda s, qi, kj: (kj[s], 0)),
                  pl.BlockSpec((DA_WIDTH, tile), lambda s, qi, kj: (0, kj[s])),
                  pl.BlockSpec((4, DA_DIM), lambda s, qi, kj: (0, 0)),
                  pl.BlockSpec((LANES, 1), lambda s, qi, kj: (0, 0))],
        out_specs=pl.BlockSpec((tile, DA_WIDTH), lambda s, qi, kj: (qi[s], 0)),
        scratch_shapes=[pltpu.VMEM((2 * DA_HEADS, 1, tile), F32),
                        pltpu.VMEM((2 * DA_HEADS, LANES + SUM_ROWS, tile), F32)])
    return pl.pallas_call(
        functools.partial(_da_flash_kernel, lam_init),
        out_shape=jax.ShapeDtypeStruct((t, DA_WIDTH), BF16),
        grid_spec=grid_spec,
        compiler_params=_cparams("arbitrary"),
    )(qi_of, kj_of, q, k, vt, lam_params, subln_g.reshape(LANES, 1))


GDN_GROUP = 256


def _softplus(x):
    return jnp.maximum(x, 0.0) + jnp.log1p(jnp.exp(-jnp.abs(x)))


def _gdn_gates_kernel(b_ref, a_ref, at_ref, alr_ref, dtr_ref, alc_ref, dtc_ref,
                      beta_ref, gcum_ref, grev_ref, gcumt_ref, egt_ref):
    n = b_ref.shape[0]
    r = lax.broadcasted_iota(jnp.int32, (n, n), 0)
    c = lax.broadcasted_iota(jnp.int32, (n, n), 1)
    same = (r // GDN_CHUNK) == (c // GDN_CHUNK)
    incl = jnp.where(same & (c <= r), 1.0, 0.0)
    rev = jnp.where(same & (c > r), 1.0, 0.0)
    beta_ref[...] = jax.nn.sigmoid(b_ref[...])
    g = -jnp.exp(alr_ref[...]) * _softplus(a_ref[...] + dtr_ref[...])
    gcum_ref[...] = jnp.dot(incl, g, precision=HIGHEST, preferred_element_type=F32)
    grev_ref[...] = jnp.dot(rev, g, precision=HIGHEST, preferred_element_type=F32)
    gt = -jnp.exp(alc_ref[...]) * _softplus(at_ref[...] + dtc_ref[...])
    gcumt = lax.dot_general(gt, incl, _NT, precision=HIGHEST, preferred_element_type=F32)
    gcumt_ref[...] = gcumt
    egt_ref[...] = jnp.exp(gcumt)


def gdn_gates(b_in, a_in, a_t, a_log, dt_bias):
    t, h = b_in.shape
    n = _tile(t, GDN_GROUP)
    colspec = pl.BlockSpec((n, h), lambda i: (i, 0))
    rowspec = pl.BlockSpec((h, n), lambda i: (0, i))
    vr = pl.BlockSpec((1, h), lambda i: (0, 0))
    vc = pl.BlockSpec((h, 1), lambda i: (0, 0))
    cs = jax.ShapeDtypeStruct((t, h), F32)
    rs = jax.ShapeDtypeStruct((h, t), F32)
    return pl.pallas_call(
        _gdn_gates_kernel,
        out_shape=(cs, cs, cs, rs, rs),
        grid=(t // n,),
        in_specs=[colspec, colspec, rowspec, vr, vr, vc, vc],
        out_specs=(colspec, colspec, colspec, rowspec, rowspec),
        compiler_params=_cparams("parallel"),
    )(b_in, a_in, a_t, a_log.reshape(1, h), dt_bias.reshape(1, h), a_log.reshape(h, 1), dt_bias.reshape(h, 1))


def _gdn_local_kernel(xq_ref, xk_ref, xv_ref, hq_ref, hk_ref, hv_ref, wq_ref, wk_ref, wv_ref,
                      beta_ref, gcum_ref, grev_ref, gcumt_ref,
                      u_ref, w_ref, qd_ref, kdt_ref, attn_ref):
    i, hg = pl.program_id(0), pl.program_id(1)
    n = xq_ref.shape[0]
    nh = xq_ref.shape[1] // LANES

    def conv_silu(x_ref, halo_ref, cw_ref, hl):
        x, cw = x_ref[:, hl], cw_ref[:, hl]
        halo = jnp.where(i > 0, halo_ref[:, hl], 0.0)

        def taps(z):
            return (cw[3:4] * z + cw[2:3] * pltpu.roll(z, 1, 0) + cw[1:2] * pltpu.roll(z, 2, 0)
                    + cw[0:1] * pltpu.roll(z, 3, 0))

        top = taps(jnp.concatenate([halo, x[:8]], axis=0))[8:16]
        y = jnp.concatenate([top, taps(x)[8:]], axis=0)
        return y * jax.nn.sigmoid(y)

    def l2n(y):
        return y * lax.rsqrt(jnp.sum(y * y, axis=-1, keepdims=True) + NORM_EPS)

    r = lax.broadcasted_iota(jnp.int32, (n, n), 0)
    c = lax.broadcasted_iota(jnp.int32, (n, n), 1)
    same = (r // GDN_CHUNK) == (c // GDN_CHUNK)
    heads = range(nh)
    hls = [slice(hb * LANES, (hb + 1) * LANES) for hb in heads]
    pws, tinvs, rest = [], [], []
    for hb in heads:
        hl, h = hls[hb], hg * nh + hb
        q = l2n(conv_silu(xq_ref, hq_ref, wq_ref, hl)) * (GDN_DK ** -0.5)
        k = l2n(conv_silu(xk_ref, hk_ref, wk_ref, hl))
        v = conv_silu(xv_ref, hv_ref, wv_ref, hl)
        head = lax.broadcasted_iota(jnp.int32, (1, GDN_HEADS), 1) == h
        pick = lambda ref: jnp.sum(jnp.where(head, ref[...], 0.0), axis=-1, keepdims=True)
        beta, gc, grev = pick(beta_ref), pick(gcum_ref), pick(grev_ref)
        gr = gcumt_ref[pl.ds(h, 1), :]
        decay = jnp.exp(jnp.where(same & (c <= r), gc - gr, NEG_BIG))
        kb = k * beta
        k16 = k.astype(BF16)
        kk = lax.dot_general(kb.astype(BF16), k16, _NT, preferred_element_type=F32)
        qk = lax.dot_general(q.astype(BF16), k16, _NT, preferred_element_type=F32)
        pw = -jnp.where(same & (c < r), kk * decay, 0.0)
        pws.append(pw)
        tinvs.append(jnp.where(r == c, 1.0, 0.0) + pw)
        attn = qk * decay
        qd_ref[:, hl] = (q * jnp.exp(gc)).astype(BF16)
        kdt_ref[hl, :] = (k * jnp.exp(grev)).T.astype(BF16)
        left = attn[:, :LANES]
        for j in range(1, n // LANES):
            left = left + attn[:, j * LANES:(j + 1) * LANES]
        attn_ref[:, hl] = (left + pltpu.roll(left, 64, 1)).astype(BF16)
        rest.append(((v * beta).astype(BF16), (kb * jnp.exp(gc)).astype(BF16)))

    for _ in range(5):
        for hb in heads:
            p16 = pws[hb].astype(BF16)
            pws[hb] = jnp.dot(p16, p16, preferred_element_type=F32)
            tinvs[hb] = tinvs[hb] + jnp.dot(tinvs[hb].astype(BF16), pws[hb].astype(BF16), preferred_element_type=F32)
    for hb in heads:
        t16 = tinvs[hb].astype(BF16)
        u_ref[:, hls[hb]] = jnp.dot(t16, rest[hb][0], preferred_element_type=F32)
        w_ref[:, hls[hb]] = jnp.dot(t16, rest[hb][1], preferred_element_type=F32).astype(BF16)


GDN_LOCAL_HEADS = 4


def gdn_local(proj, qblk, conv_w, beta, gcum, grev, gcumt):
    t = proj.shape[0]
    n = _tile(t, GDN_GROUP)
    nh, hl = GDN_HEADS, GDN_LOCAL_HEADS
    wide = hl * LANES
    assert qblk % hl == 0 and nh % hl == 0
    x = lambda part: pl.BlockSpec((n, wide), lambda i, h, part=part: (i, (qblk + part * nh) // hl + h))
    halo = lambda part: pl.BlockSpec(
        (8, wide), lambda i, h, part=part: (jnp.maximum(i * (n // 8) - 1, 0), (qblk + part * nh) // hl + h))
    cw = lambda part: pl.BlockSpec((GDN_CONV, wide), lambda i, h, part=part: (0, part * nh // hl + h))
    col = pl.BlockSpec((n, nh), lambda i, h: (i, 0))
    row = pl.BlockSpec((nh, n), lambda i, h: (0, i))
    out = pl.BlockSpec((n, wide), lambda i, h: (i, h))
    f32o = jax.ShapeDtypeStruct((t, GDN_WIDTH), F32)
    b16o = jax.ShapeDtypeStruct((t, GDN_WIDTH), BF16)
    return pl.pallas_call(
        _gdn_local_kernel,
        out_shape=(f32o, b16o, b16o, jax.ShapeDtypeStruct((GDN_WIDTH, t), BF16), b16o),
        grid=(t // n, nh // hl),
        in_specs=[x(0), x(1), x(2), halo(0), halo(1), halo(2), cw(0), cw(1), cw(2), col, col, col, row],
        out_specs=(out, out, out, pl.BlockSpec((wide, n), lambda i, h: (h, i)), out),
        compiler_params=_cparams("parallel", "parallel"),
    )(proj, proj, proj, proj, proj, proj, conv_w, conv_w, conv_w, beta, gcum, grev, gcumt)


GDN_SCAN_HEADS = 4


def _gdn_scan_kernel(egl_ref, u_ref, w_ref, qd_ref, kdt_ref, attn_ref, z_ref, g_ref, o_ref, s_ref):
    hg, i = pl.program_id(0), pl.program_id(1)
    n = u_ref.shape[0]
    cpg = n // GDN_CHUNK
    nh = s_ref.shape[0]

    @pl.when(i == 0)
    def _():
        s_ref[...] = jnp.zeros(s_ref.shape, F32)

    states = [s_ref[hb] for hb in range(nh)]
    zeros = jnp.zeros((GDN_CHUNK, GDN_DV), BF16)
    for c in range(cpg):
        sl = slice(c * GDN_CHUNK, (c + 1) * GDN_CHUNK)
        pair = slice((c // 2) * LANES, (c // 2 + 1) * LANES)
        for hb in range(nh):
            hl = slice(hb * LANES, (hb + 1) * LANES)
            s16 = states[hb].astype(BF16)
            vnew = u_ref[sl, hl] - jnp.dot(w_ref[sl, hl], s16, preferred_element_type=F32)
            v16 = vnew.astype(BF16)
            o = (jnp.dot(qd_ref[sl, hl], s16, preferred_element_type=F32)
                 + jnp.dot(attn_ref[sl, hb * LANES:hb * LANES + GDN_CHUNK], v16, preferred_element_type=F32))
            vpad = jnp.concatenate([v16, zeros] if c % 2 == 0 else [zeros, v16], axis=0)
            states[hb] = (states[hb] * egl_ref[hg * nh + hb, i * cpg + c]
                          + jnp.dot(kdt_ref[hl, pair], vpad, preferred_element_type=F32))
            y = o * lax.rsqrt(jnp.mean(o * o, axis=-1, keepdims=True) + NORM_EPS) * g_ref[...]
            z = z_ref[sl, hl]
            o_ref[sl, hl] = (y * (z * jax.nn.sigmoid(z))).astype(o_ref.dtype)
    for hb in range(nh):
        s_ref[hb] = states[hb]


def gdn_scan(egl, u, w, qd, kdt, attn, proj, zblk, norm_g):
    t = u.shape[0]
    n = _tile(t, GDN_GROUP)
    nh = GDN_SCAN_HEADS
    wide = nh * LANES
    assert zblk % nh == 0 and GDN_HEADS % nh == 0
    blk = pl.BlockSpec((n, wide), lambda h, i: (i, h))
    return pl.pallas_call(
        _gdn_scan_kernel,
        out_shape=jax.ShapeDtypeStruct((t, GDN_WIDTH), BF16),
        grid=(GDN_HEADS // nh, t // n),
        in_specs=[pl.BlockSpec(memory_space=pltpu.SMEM), blk, blk, blk,
                  pl.BlockSpec((wide, n), lambda h, i: (h, i)), blk,
                  pl.BlockSpec((n, wide), lambda h, i: (i, zblk // nh + h)),
                  pl.BlockSpec((1, LANES), lambda h, i: (0, 0))],
        out_specs=blk,
        scratch_shapes=[pltpu.VMEM((nh, GDN_DK, GDN_DV), F32)],
        compiler_params=_cparams("parallel", "arbitrary"),
    )(egl, u, w, qd, kdt, attn, proj, norm_g.reshape(1, LANES))


def gated_deltanet(proj, small_b, small_a, qblk, zblk, conv_w, a_log, dt_bias, norm_g):
    beta, gcum, grev, gcumt, egt = gdn_gates(small_b, small_a, small_a.T, a_log, dt_bias)
    u, w, qd, kdt, attn = gdn_local(proj, qblk, conv_w, beta, gcum, grev, gcumt)
    egl = egt[:, GDN_CHUNK - 1::GDN_CHUNK]
    return gdn_scan(egl, u, w, qd, kdt, attn, proj, zblk, norm_g)


DSA_SEL_TQ = 128
DSA_SEL_TK = 256


def _dsa_prep_kernel(q_ref, k_ref, v_ref, qi_ref, ki_ref, wi_ref, cos64_ref, sin64_ref, cos128_ref, sin128_ref,
                     qg_ref, kg_ref, qo_ref, ko_ref, vo_ref, qio_ref, kio_ref, wio_ref):
    lane = lax.broadcasted_iota(jnp.int32, (1, LANES), 1)
    cos64, sin64 = cos64_ref[...], sin64_ref[...]
    cos128, sin128 = cos128_ref[...], sin128_ref[...]
    for c in range(DSA_HEADS):
        sl = slice(c * LANES, (c + 1) * LANES)
        for x_ref, g_ref, o_ref, scale in ((q_ref, qg_ref, qo_ref, DSA_DIM ** -0.5 * LOG2E), (k_ref, kg_ref, ko_ref, 1.0)):
            x = x_ref[:, sl]
            y = x * lax.rsqrt(jnp.mean(x * x, axis=-1, keepdims=True) + NORM_EPS) * g_ref[...]
            o_ref[:, sl] = (_rope128(y, cos128, sin128) * scale).astype(BF16)
    vo_ref[...] = v_ref[...].T.astype(BF16)
    tq = DSA_SEL_TQ
    for c in range(IDX_HEADS * IDX_DIM // LANES):
        y = _rope64(qi_ref[:, c * LANES:(c + 1) * LANES], cos64, sin64, lane).astype(BF16)
        for b in range(qi_ref.shape[0] // tq):
            for half in range(2):
                h = 2 * c + half
                qio_ref[b, h * tq:(h + 1) * tq, :] = y[b * tq:(b + 1) * tq, half * IDX_DIM:(half + 1) * IDX_DIM]
    kio_ref[...] = _rope64(ki_ref[...], cos64, sin64, lane).astype(BF16)
    wio_ref[...] = wi_ref[...] * ((IDX_HEADS * IDX_DIM) ** -0.5)


def dsa_prep(proj, qblk, qiblk, ki, wi, tabs64, tabs128, qg, kg):
    t = proj.shape[0]
    tm = _tile(t, 256)
    w = DSA_WIDTH
    nb = tm // DSA_SEL_TQ
    col = lambda c: pl.BlockSpec((tm, w), lambda i, c=c: (i, c))
    row = pl.BlockSpec((tm, LANES), lambda i: (i, 0))
    vec = pl.BlockSpec((1, LANES), lambda i: (0, 0))
    out = pl.BlockSpec((tm, w), lambda i: (i, 0))
    shp = jax.ShapeDtypeStruct((t, w), BF16)
    return pl.pallas_call(
        _dsa_prep_kernel,
        out_shape=(shp, shp, jax.ShapeDtypeStruct((w, t), BF16),
                   jax.ShapeDtypeStruct((t // DSA_SEL_TQ, IDX_HEADS * DSA_SEL_TQ, IDX_DIM), BF16),
                   jax.ShapeDtypeStruct((t, LANES), BF16), jax.ShapeDtypeStruct((t, IDX_HEADS), F32)),
        grid=(t // tm,),
        in_specs=[col(qblk), col(qblk + 1), col(qblk + 2),
                  pl.BlockSpec((tm, IDX_HEADS * IDX_DIM), lambda i: (i, qiblk)), row,
                  pl.BlockSpec((tm, IDX_HEADS), lambda i: (i, 0)), row, row, row, row, vec, vec],
        out_specs=(out, out, pl.BlockSpec((w, tm), lambda i: (0, i)),
                   pl.BlockSpec((nb, IDX_HEADS * DSA_SEL_TQ, IDX_DIM), lambda i: (i, 0, 0)), row,
                   pl.BlockSpec((tm, IDX_HEADS), lambda i: (i, 0))),
        compiler_params=_cparams("parallel"),
    )(proj, proj, proj, proj, ki, wi, *tabs64, *tabs128, qg.reshape(1, LANES), kg.reshape(1, LANES))


def _order_key(x):
    bits = pltpu.bitcast(x, jnp.int32)
    return jnp.where(bits < 0, bits ^ 0x7FFFFFFF, bits)


def _order_value(key):
    return pltpu.bitcast(jnp.where(key < 0, key ^ 0x7FFFFFFF, key), F32)


def _dsa_select_kernel(n_sel, qi_ref, ki_ref, wt_ref, mask_ref, keys_sc):
    i = pl.program_id(0)
    tq, tk = DSA_SEL_TQ, DSA_SEL_TK
    t = ki_ref.shape[0]
    nvalid = ((i + 1) * tq + tk - 1) // tk
    qpos = i * tq + lax.broadcasted_iota(jnp.int32, (1, tq), 1)
    w = wt_ref[...]
    q_all = qi_ref[0]
    chunk = lambda c: pl.ds(pl.multiple_of(c * tk, tk), tk)

    def score_chunk(c, carry):
        lg = lax.dot_general(ki_ref[chunk(c), :IDX_DIM], q_all, _NT, preferred_element_type=F32)
        acc = jnp.zeros((tk, tq), F32)
        for h in range(IDX_HEADS):
            acc = acc + w[h:h + 1, :] * jnp.maximum(lg[:, h * tq:(h + 1) * tq], 0.0)
        key = _order_key(acc)
        kpos = c * tk + lax.broadcasted_iota(jnp.int32, (tk, 1), 0)
        causal = kpos <= qpos
        keys_sc[chunk(c), :] = jnp.where(causal, key, INT_MIN)
        kmin8, kmax8, sum8, sq8 = carry
        fold = lambda x: x.reshape(tk // 8, 8, tq)
        live = jnp.where(causal, acc, 0.0)
        return (jnp.minimum(kmin8, jnp.min(fold(jnp.where(causal, key, INT_MAX)), axis=0)),
                jnp.maximum(kmax8, jnp.max(fold(jnp.where(causal, key, INT_MIN)), axis=0)),
                sum8 + jnp.sum(fold(live), axis=0), sq8 + jnp.sum(fold(live * live), axis=0))

    zeros8 = jnp.zeros((8, tq), F32)
    kmin8, kmax8, sum8, sq8 = lax.fori_loop(0, nvalid, score_chunk, (
        jnp.full((8, tq), INT_MAX, jnp.int32), jnp.full((8, tq), INT_MIN, jnp.int32), zeros8, zeros8))

    def count_ge(thr):
        def body(c, acc):
            hit = jnp.where(keys_sc[chunk(c), :] >= thr, 1, 0)
            return acc + jnp.sum(hit.reshape(tk // 8, 8, tq), axis=0)

        acc = lax.fori_loop(0, nvalid, body, jnp.zeros((8, tq), jnp.int32))
        return jnp.sum(acc, axis=0, keepdims=True)

    lo0 = jnp.min(kmin8, axis=0, keepdims=True)
    hi0 = jnp.max(kmax8, axis=0, keepdims=True) + 1
    cnt_lo0 = qpos + 1
    cnt_hi0 = jnp.zeros((1, tq), jnp.int32)

    def unsettled(lo, hi, cnt_lo):
        return jnp.max(jnp.where((cnt_lo > n_sel) & (lo + 1 != hi), 1, 0))

    n_live = cnt_lo0.astype(F32)
    mean = jnp.sum(sum8, axis=0, keepdims=True) / n_live
    dev = jnp.sqrt(jnp.maximum(jnp.sum(sq8, axis=0, keepdims=True) / n_live - mean * mean, 0.0))
    tail = jnp.clip(n_sel / n_live, 1e-6, 0.5)
    tt = jnp.sqrt(-2.0 * jnp.log(tail))
    z = tt - (2.30753 + 0.27061 * tt) / (1.0 + 0.99229 * tt + 0.04481 * tt * tt)
    slope = dev / (z + 1.0 / jnp.maximum(z, 0.5))
    log_n = math.log(n_sel)

    def narrow(carry):
        lo, hi, cnt_lo, cnt_hi, w_lo, w_hi, last_ok, it, _ = carry
        half = (lo >> 1) + (hi >> 1) + (lo & hi & 1)
        v_lo, v_hi = _order_value(lo), _order_value(hi)
        e_lo, e_hi = jnp.log(cnt_lo.astype(F32) + 0.5) - log_n, jnp.log(cnt_hi.astype(F32) + 0.5) - log_n
        f_lo, f_hi = e_lo * w_lo, e_hi * w_hi
        false_pos = v_lo + f_lo / (f_lo - f_hi) * (v_hi - v_lo)
        along_tail = jnp.where(last_ok > 0, v_lo + e_lo * slope, v_hi + e_hi * slope)
        guess = _order_key(jnp.where(it == 0, mean + z * dev, jnp.where(it == 1, along_tail, false_pos)))
        mid = jnp.where((guess > lo) & (guess < hi) & ((it & 7) != 7), guess, half)
        cnt = count_ge(mid)
        ok = cnt >= n_sel
        w_hi = jnp.where(ok, jnp.where(last_ok > 0, 0.5 * w_hi, w_hi), 1.0)
        w_lo = jnp.where(ok, 1.0, jnp.where(last_ok < 0, 0.5 * w_lo, w_lo))
        lo, cnt_lo = jnp.where(ok, mid, lo), jnp.where(ok, cnt, cnt_lo)
        hi, cnt_hi = jnp.where(ok, hi, mid), jnp.where(ok, cnt_hi, cnt)
        return lo, hi, cnt_lo, cnt_hi, w_lo, w_hi, jnp.where(ok, 1, -1), it + 1, unsettled(lo, hi, cnt_lo)

    ones = jnp.ones((1, tq), F32)
    lo = lax.while_loop(lambda cr: cr[8] > 0, narrow,
                        (lo0, hi0, cnt_lo0, cnt_hi0, ones, ones, jnp.zeros((1, tq), jnp.int32), jnp.int32(0),
                         unsettled(lo0, hi0, cnt_lo0)))[0]

    def write_valid(c, carry):
        mask_ref[chunk(c), :] = jnp.where(keys_sc[chunk(c), :] >= lo, 1.0, 0.0).astype(mask_ref.dtype)
        return carry

    def write_zero(c, carry):
        mask_ref[chunk(c), :] = jnp.zeros((tk, tq), mask_ref.dtype)
        return carry

    lax.fori_loop(0, nvalid, write_valid, 0)
    lax.fori_loop(nvalid, t // tk, write_zero, 0)


def dsa_select(qi, ki, wt, n_sel):
    t = ki.shape[0]
    tq = DSA_SEL_TQ
    return pl.pallas_call(
        functools.partial(_dsa_select_kernel, n_sel),
        out_shape=jax.ShapeDtypeStruct((t, t), BF16),
        grid=(t // tq,),
        in_specs=[pl.BlockSpec((1, IDX_HEADS * tq, IDX_DIM), lambda i: (i, 0, 0)),
                  pl.BlockSpec((t, LANES), lambda i: (0, 0)),
                  pl.BlockSpec((IDX_HEADS, tq), lambda i: (0, i))],
        out_specs=pl.BlockSpec((t, tq), lambda i: (0, i)),
        scratch_shapes=[pltpu.VMEM((t, tq), jnp.int32)],
        compiler_params=_cparams("parallel"),
    )(qi, ki, wt)


def _dsa_flash_kernel(qi_of, kj_of, q_ref, k_ref, vt_ref, mask_ref, o_ref, m_sc, acc_sc):
    step = pl.program_id(0)
    qi, kj = qi_of[step], kj_of[step]

    @pl.when(kj == 0)
    def _():
        _init_softmax_stats(m_sc, acc_sc)

    def one_head(h, carry):
        c = pl.ds(pl.multiple_of(h * LANES, LANES), LANES)
        st = lax.dot_general(k_ref[:, c], q_ref[:, c], _NT, preferred_element_type=F32)
        st = jnp.where(mask_ref[...] > 0, st, NEG_BIG)
        _softmax_step_t(st, _with_sum_rows(vt_ref[c, :]), h, m_sc, acc_sc)
        return carry

    lax.fori_loop(0, DSA_HEADS, one_head, 0)

    @pl.when(kj == qi)
    def _():
        def finish(h, carry):
            o_ref[:, pl.ds(pl.multiple_of(h * LANES, LANES), LANES)] = _softmax_result_t(acc_sc[h]).T.astype(o_ref.dtype)
            return carry

        lax.fori_loop(0, DSA_HEADS, finish, 0)


def dsa_flash(q, k, vt, mask_t):
    t = q.shape[0]
    tile = _tile(t, ATTN_TILE)
    qi_of, kj_of = _causal_steps(t, tile)
    grid_spec = pltpu.PrefetchScalarGridSpec(
        num_scalar_prefetch=2,
        grid=(qi_of.shape[0],),
        in_specs=[pl.BlockSpec((tile, DSA_WIDTH), lambda s, qi, kj: (qi[s], 0)),
                  pl.BlockSpec((tile, DSA_WIDTH), lambda s, qi, kj: (kj[s], 0)),
                  pl.BlockSpec((DSA_WIDTH, tile), lambda s, qi, kj: (0, kj[s])),
                  pl.BlockSpec((tile, tile), lambda s, qi, kj: (kj[s], qi[s]))],
        out_specs=pl.BlockSpec((tile, DSA_WIDTH), lambda s, qi, kj: (qi[s], 0)),
        scratch_shapes=[pltpu.VMEM((DSA_HEADS, 1, tile), F32),
                        pltpu.VMEM((DSA_HEADS, LANES + SUM_ROWS, tile), F32)])
    return pl.pallas_call(
        _dsa_flash_kernel,
        out_shape=jax.ShapeDtypeStruct((t, DSA_WIDTH), BF16),
        grid_spec=grid_spec,
        compiler_params=_cparams("arbitrary"),
    )(qi_of, kj_of, q, k, vt, mask_t)


def dsa_attention(proj, qblk, qiblk, ki, wi, tabs64, tabs128, qg, kg):
    t = proj.shape[0]
    q, k, vt, qi, kir, wis = dsa_prep(proj, qblk, qiblk, ki, wi, tabs64, tabs128, qg, kg)
    mask_t = dsa_select(qi, kir, wis.T, min(TOPK_MAX, t // 4))
    return dsa_flash(q, k, vt, mask_t)


def _take_top(s, count):
    n = s.shape[0]
    idx = lax.broadcasted_iota(jnp.int32, s.shape, 0)
    vals = []
    for _ in range(count):
        m = jnp.max(s, axis=0, keepdims=True)
        first = jnp.min(jnp.where(s == m, idx, n), axis=0, keepdims=True)
        s = jnp.where(idx == first, NEG_BIG, s)
        vals.append(m)
    return vals


def _peer_route_kernel(q_ref, sk_ref, s1_ref, s2_ref, e1_ref, e2_ref, tau_ref):
    for h in range(PEER_HEADS):
        halves = []
        for c in range(2):
            sl = slice((2 * h + c) * LANES, (2 * h + c + 1) * LANES)
            halves.append(lax.dot_general(sk_ref[h, c], q_ref[:, sl].astype(BF16), _NT, preferred_element_type=F32))
        s1, s2 = halves
        v1, v2 = _take_top(s1, PEER_TOPK), _take_top(s2, PEER_TOPK)
        v1_all, v2_all = jnp.concatenate(v1, axis=0), jnp.concatenate(v2, axis=0)
        cand = jnp.concatenate([v1[0] + v2_all] + [v1[a] + v2_all[:8] for a in range(1, 8)] + [v1_all[8:] + v2[0]],
                               axis=0)
        top = _take_top(cand, PEER_TOPK)
        z = sum(jnp.exp(tv - top[0]) for tv in top)
        s1_ref[h], s2_ref[h] = s1, s2
        e1_ref[h] = jnp.exp(s1 - v1[0])
        e2_ref[h] = jnp.exp(s2 - v2[0]) / z
        tau_ref[h:h + 1, :] = top[-1]


def peer_route(q, sub_keys):
    t = q.shape[0]
    tm = _tile(t, 256)
    big = pl.BlockSpec((PEER_HEADS, PEER_NKEYS, tm), lambda i: (0, 0, i))
    bs = jax.ShapeDtypeStruct((PEER_HEADS, PEER_NKEYS, t), F32)
    return pl.pallas_call(
        _peer_route_kernel,
        out_shape=(bs, bs, bs, bs, jax.ShapeDtypeStruct((PEER_HEADS, t), F32)),
        grid=(t // tm,),
        in_specs=[pl.BlockSpec((tm, PEER_HEADS * PEER_DKEY), lambda i: (i, 0)),
                  pl.BlockSpec(sub_keys.shape, lambda i: (0, 0, 0, 0))],
        out_specs=(big, big, big, big, pl.BlockSpec((PEER_HEADS, tm), lambda i: (0, i))),
        compiler_params=_cparams("parallel"),
    )(q, sub_keys)


def _gelu_tanh(x):
    return 0.5 * x * (1.0 + jnp.tanh(math.sqrt(2.0 / math.pi) * (x + 0.044715 * (x * x * x))))


def _peer_expert_kernel(x_ref, u_ref, v_ref, s1_ref, s2_ref, e1_ref, e2_ref, tau_ref, o_ref):
    j = pl.program_id(1)
    te = u_ref.shape[0]

    @pl.when(j == 0)
    def _():
        o_ref[...] = jnp.zeros(o_ref.shape, F32)

    act = _gelu_tanh(lax.dot_general(u_ref[...], x_ref[...], _NT, preferred_element_type=F32))
    rows_per = te // PEER_NKEYS
    gates = []
    for a in range(rows_per):
        n1 = j * rows_per + a
        g = None
        for h in range(PEER_HEADS):
            pair = s1_ref[h, pl.ds(n1, 1), :] + s2_ref[h]
            gh = jnp.where(pair >= tau_ref[h:h + 1, :], e1_ref[h, pl.ds(n1, 1), :] * e2_ref[h], 0.0)
            g = gh if g is None else g + gh
        gates.append(g)
    ga = (jnp.concatenate(gates, axis=0) * act).astype(BF16)
    o_ref[...] += lax.dot_general(ga, v_ref[...], (((0,), (0,)), ((), ())), preferred_element_type=F32)


def peer_expert(x, u, v, s1, s2, e1, e2, tau, tm=512, te=512):
    t, d = x.shape
    e = u.shape[0]
    tm, te = _tile(t, tm), _tile(e, te)
    once = pl.Buffered(1)
    big = pl.BlockSpec((PEER_HEADS, PEER_NKEYS, tm), lambda i, j: (0, 0, i), pipeline_mode=once)
    return pl.pallas_call(
        _peer_expert_kernel,
        out_shape=jax.ShapeDtypeStruct((t, d), F32),
        grid=(t // tm, e // te),
        in_specs=[pl.BlockSpec((tm, d), lambda i, j: (i, 0), pipeline_mode=once),
                  pl.BlockSpec((te, d), lambda i, j: (j, 0)),
                  pl.BlockSpec((te, d), lambda i, j: (j, 0)),
                  big, big, big, big,
                  pl.BlockSpec((PEER_HEADS, tm), lambda i, j: (0, i), pipeline_mode=once)],
        out_specs=pl.BlockSpec((tm, d), lambda i, j: (i, 0)),
        compiler_params=_cparams("parallel", "arbitrary"),
    )(x, u, v, s1, s2, e1, e2, tau)


def _add_rmsnorm_kernel(h_ref, y_ref, g_ref, ho_ref, ao_ref):
    x = h_ref[...] + y_ref[...]
    ho_ref[...] = x
    ao_ref[...] = (x * lax.rsqrt(jnp.mean(x * x, axis=-1, keepdims=True) + NORM_EPS) * g_ref[...]).astype(BF16)


def add_rmsnorm(h, y, g):
    t, d = h.shape
    tm = _tile(t, 256)
    blk = pl.BlockSpec((tm, d), lambda i: (i, 0))
    return pl.pallas_call(
        _add_rmsnorm_kernel,
        out_shape=(jax.ShapeDtypeStruct((t, d), F32), jax.ShapeDtypeStruct((t, d), BF16)),
        grid=(t // tm,),
        in_specs=[blk, blk, pl.BlockSpec((1, d), lambda i: (0, 0))],
        out_specs=(blk, blk),
        compiler_params=_cparams("parallel"),
    )(h, y, g.reshape(1, d))


def _ple_kernel(a_ref, wg_ref, p_ref, wp_ref, h_ref, o_ref):
    gate = jax.nn.sigmoid(jnp.dot(a_ref[...], wg_ref[...], preferred_element_type=F32))
    o_ref[...] = h_ref[...] + gate * jnp.dot(p_ref[...], wp_ref[...], preferred_element_type=F32)


def ple(a, wg, p, wp, h, tm=512, tn=1024):
    m, k = a.shape
    n = wg.shape[1]
    kp = p.shape[1]
    tm, tn = _tile(m, tm), _tile(n, tn)
    return pl.pallas_call(
        _ple_kernel,
        out_shape=jax.ShapeDtypeStruct((m, n), F32),
        grid=(n // tn, m // tm),
        in_specs=[pl.BlockSpec((tm, k), lambda j, i: (i, 0)), pl.BlockSpec((k, tn), lambda j, i: (0, j)),
                  pl.BlockSpec((tm, kp), lambda j, i: (i, 0)), pl.BlockSpec((kp, tn), lambda j, i: (0, j)),
                  pl.BlockSpec((tm, tn), lambda j, i: (i, j))],
        out_specs=pl.BlockSpec((tm, tn), lambda j, i: (i, j)),
        compiler_params=_cparams("parallel", "parallel"),
    )(a, wg, p, wp, h)


_QBLK_GDN, _ZBLK_GDN = 0, 48
_BLK_DA = 8
_BLK_DSA = 11
_BLK_IDX = 7


def _regroup_w_in(w):
    sizes = (DA_WIDTH, DA_WIDTH, DA_WIDTH, 3 * GDN_WIDTH, GDN_WIDTH, GDN_HEADS, GDN_HEADS,
             DSA_WIDTH, DSA_WIDTH, DSA_WIDTH, IDX_HEADS * IDX_DIM, IDX_DIM, IDX_HEADS)
    offs = [0]
    for s in sizes:
        offs.append(offs[-1] + s)
    (da_q, da_k, da_v, g_qkv, g_z, g_b, g_a, c_q, c_k, c_v, c_qi, c_ki, c_w) = [
        w[:, offs[n]:offs[n + 1]] for n in range(len(sizes))]
    main = jnp.concatenate([g_qkv, g_z, da_q, da_k, da_v, c_q, c_k, c_v, c_qi], axis=1).astype(BF16)
    small = jnp.concatenate([c_ki, c_w, g_b, g_a], axis=1).astype(BF16)
    return main, small


def kernel(x, p, attn_norm, w_in, da_q_norm, da_k_norm, da_lambda, da_subln, gdn_conv, gdn_a_log, gdn_dt_bias, gdn_norm, dsa_q_norm, dsa_k_norm, w_out, ffn_norm, peer_w_q, peer_sub_keys, peer_u, peer_v, ple_norm, w_ple_gate, w_ple_proj):
    b, t, d = x.shape
    assert b == 1
    depth = w_in.shape[0]
    tabs64, tabs128 = _rope_tables(t, 64), _rope_tables(t, 128)
    h = x.reshape(t, d)
    for i in range(depth):
        w_main, w_small = _regroup_w_in(w_in[i])
        a = rmsnorm_bf16(h, attn_norm[i])
        proj = matmul(a, w_main)
        small = matmul(a, w_small)
        lam_init = 0.8 - 0.6 * math.exp(-0.3 * i)
        qa, ka, va = da_prep(proj, _BLK_DA, *tabs64, da_q_norm[i], da_k_norm[i])
        o_a = da_flash(qa, ka, va, da_lambda[i], da_subln[i], lam_init)
        o_b = gated_deltanet(proj, small[:, 96:112], small[:, 112:128], _QBLK_GDN, _ZBLK_GDN, gdn_conv[i],
                             gdn_a_log[i], gdn_dt_bias[i], gdn_norm[i])
        o_c = dsa_attention(proj, _BLK_DSA, _BLK_IDX, small, small[:, 64:96], tabs64, tabs128,
                            dsa_q_norm[i], dsa_k_norm[i])
        mix = jnp.concatenate([o_a, o_b, o_c], axis=-1)
        h = matmul(mix, w_out[i].astype(BF16), residual=h)

        a = rmsnorm_bf16(h, ffn_norm[i])
        pq = matmul(a, peer_w_q[i].astype(BF16))
        s1, s2, e1, e2, tau = peer_route(pq, peer_sub_keys[i].astype(BF16))
        y = peer_expert(a, peer_u[i].astype(BF16), peer_v[i].astype(BF16), s1, s2, e1, e2, tau)
        h, a = add_rmsnorm(h, y, ple_norm[i])
        h = ple(a, w_ple_gate[i].astype(BF16), p[i, 0].astype(BF16), w_ple_proj[i].astype(BF16), h)
    return h.reshape(b, t, d)
```

```python
import functools
import math

import jax
import jax.numpy as jnp
from jax import lax
from jax.experimental import pallas as pl
from jax.experimental.pallas import tpu as pltpu

F32 = jnp.float32
BF16 = jnp.bfloat16
HIGHEST = lax.Precision.HIGHEST

NORM_EPS = 1e-6
ROPE_THETA = 10000.0
LANES = 128
VMEM_LIMIT = 56 * 1024 * 1024

DA_HEADS, DA_DIM = 8, 64
DA_WIDTH = DA_HEADS * 2 * DA_DIM
GDN_HEADS, GDN_DK, GDN_DV, GDN_CONV, GDN_CHUNK = 16, 128, 128, 4, 64
GDN_WIDTH = GDN_HEADS * GDN_DV
DSA_HEADS, DSA_DIM = 8, 128
DSA_WIDTH = DSA_HEADS * DSA_DIM
IDX_HEADS, IDX_DIM = 32, 64
TOPK_MAX = 256
PEER_HEADS, PEER_NKEYS, PEER_DKEY, PEER_TOPK = 8, 128, 256, 16
PLE_DIM = 256

LOG2E = math.log2(math.e)
NEG_BIG = -1e30
INT_MIN = -(2 ** 31)
INT_MAX = 2 ** 31 - 1

_NT = (((1,), (1,)), ((), ()))


def _cparams(*sem):
    return pltpu.CompilerParams(dimension_semantics=sem, vmem_limit_bytes=VMEM_LIMIT)


def _tile(n, pref):
    t = min(n, pref)
    assert n % t == 0, (n, pref)
    return t


def _rmsnorm_kernel(x_ref, g_ref, o_ref):
    x = x_ref[...]
    y = x * lax.rsqrt(jnp.mean(x * x, axis=-1, keepdims=True) + NORM_EPS) * g_ref[...]
    o_ref[...] = y.astype(o_ref.dtype)


def rmsnorm_bf16(x, g):
    t, d = x.shape
    tm = _tile(t, 256)
    return pl.pallas_call(
        _rmsnorm_kernel,
        out_shape=jax.ShapeDtypeStruct((t, d), BF16),
        grid=(t // tm,),
        in_specs=[pl.BlockSpec((tm, d), lambda i: (i, 0)), pl.BlockSpec((1, d), lambda i: (0, 0))],
        out_specs=pl.BlockSpec((tm, d), lambda i: (i, 0)),
        compiler_params=_cparams("parallel"),
    )(x, g.reshape(1, d))


def _matmul_kernel(a_ref, b_ref, o_ref):
    o_ref[...] = jnp.dot(a_ref[...], b_ref[...], preferred_element_type=F32)


def _matmul_res_kernel(a_ref, b_ref, r_ref, o_ref):
    o_ref[...] = r_ref[...] + jnp.dot(a_ref[...], b_ref[...], preferred_element_type=F32)


def matmul(a, b, residual=None, tm=512, tn=1024):
    m, k = a.shape
    n = b.shape[1]
    tm, tn = _tile(m, tm), _tile(n, tn)
    in_specs = [pl.BlockSpec((tm, k), lambda j, i: (i, 0)), pl.BlockSpec((k, tn), lambda j, i: (0, j))]
    args = [a, b]
    body = _matmul_kernel
    if residual is not None:
        in_specs.append(pl.BlockSpec((tm, tn), lambda j, i: (i, j)))
        args.append(residual)
        body = _matmul_res_kernel
    return pl.pallas_call(
        body,
        out_shape=jax.ShapeDtypeStruct((m, n), F32),
        grid=(n // tn, m // tm),
        in_specs=in_specs,
        out_specs=pl.BlockSpec((tm, tn), lambda j, i: (i, j)),
        compiler_params=_cparams("parallel", "parallel"),
    )(*args)


def _rope_tables(t, d):
    pos = jnp.arange(t, dtype=F32)
    inv = ROPE_THETA ** (-jnp.arange(0, d, 2, dtype=F32) / d)
    ang = pos[:, None] * inv[None, :]
    cos, sin = jnp.cos(ang), jnp.sin(ang)
    reps = LANES // d
    return (jnp.tile(jnp.concatenate([cos, cos], -1), (1, reps)),
            jnp.tile(jnp.concatenate([-sin, sin], -1), (1, reps)))


def _rope64(y, cos, sin, lane):
    partner = jnp.where((lane & 32) == 0, pltpu.roll(y, 96, 1), pltpu.roll(y, 32, 1))
    return y * cos + partner * sin


def _rope128(y, cos, sin):
    return y * cos + pltpu.roll(y, 64, 1) * sin


def _seg64_mean_sq(x, lane):
    x2 = x * x
    lo = jnp.sum(jnp.where(lane < 64, x2, 0.0), axis=-1, keepdims=True)
    hi = jnp.sum(jnp.where(lane >= 64, x2, 0.0), axis=-1, keepdims=True)
    return jnp.where(lane < 64, lo, hi) * (1.0 / 64)


def _da_prep_kernel(q_ref, k_ref, v_ref, cos_ref, sin_ref, qg_ref, kg_ref, qo_ref, ko_ref, vo_ref):
    lane = lax.broadcasted_iota(jnp.int32, (1, LANES), 1)
    cos, sin = cos_ref[...], sin_ref[...]
    for c in range(DA_WIDTH // LANES):
        sl = slice(c * LANES, (c + 1) * LANES)
        for x_ref, g_ref, o_ref, scale in ((q_ref, qg_ref, qo_ref, DA_DIM ** -0.5 * LOG2E), (k_ref, kg_ref, ko_ref, 1.0)):
            x = x_ref[:, sl]
            y = x * lax.rsqrt(_seg64_mean_sq(x, lane) + NORM_EPS) * g_ref[...]
            y = _rope64(y, cos, sin, lane)
            o_ref[:, sl] = (y * scale).astype(BF16)
    vo_ref[...] = v_ref[...].T.astype(BF16)


def da_prep(proj, colblk, cos, sin, qg, kg):
    t = proj.shape[0]
    tm = _tile(t, 256)
    w = DA_WIDTH
    col = lambda c: pl.BlockSpec((tm, w), lambda i, c=c: (i, c))
    row = pl.BlockSpec((tm, LANES), lambda i: (i, 0))
    vec = pl.BlockSpec((1, LANES), lambda i: (0, 0))
    out = pl.BlockSpec((tm, w), lambda i: (i, 0))
    shp = jax.ShapeDtypeStruct((t, w), BF16)
    return pl.pallas_call(
        _da_prep_kernel,
        out_shape=(shp, shp, jax.ShapeDtypeStruct((w, t), BF16)),
        grid=(t // tm,),
        in_specs=[col(colblk), col(colblk + 1), col(colblk + 2), row, row, vec, vec],
        out_specs=(out, out, pl.BlockSpec((w, tm), lambda i: (0, i))),
        compiler_params=_cparams("parallel"),
    )(proj, proj, proj, cos, sin, jnp.tile(qg, 2).reshape(1, LANES), jnp.tile(kg, 2).reshape(1, LANES))


ATTN_TILE = 1024


def _causal_steps(t, tile):
    n = t // tile
    pairs = [(i, j) for i in range(n) for j in range(i + 1)]
    return (jnp.asarray([a for a, _ in pairs], jnp.int32), jnp.asarray([b for _, b in pairs], jnp.int32))


SUM_ROWS = 16


def _with_sum_rows(vt):
    return jnp.concatenate([vt, jnp.ones((SUM_ROWS, vt.shape[1]), vt.dtype)], axis=0)


def _softmax_step_t(st, vt1, idx, m_sc, acc_sc):
    m_prev = m_sc[idx]
    m_new = jnp.maximum(m_prev, jnp.max(st, axis=0, keepdims=True))
    p = jnp.exp2(st - m_new).astype(BF16)
    acc_sc[idx] = jnp.exp2(m_prev - m_new) * acc_sc[idx] + jnp.dot(vt1, p, preferred_element_type=F32)
    m_sc[idx] = m_new


def _softmax_result_t(acc):
    return acc[:LANES] / acc[LANES:LANES + 1]


def _init_softmax_stats(m_sc, acc_sc):
    m_sc[...] = jnp.full(m_sc.shape, NEG_BIG, F32)
    acc_sc[...] = jnp.zeros(acc_sc.shape, F32)


def _da_flash_kernel(lam_init, qi_of, kj_of, q_ref, k_ref, vt_ref, lp_ref, g_ref, o_ref, m_sc, acc_sc):
    step = pl.program_id(0)
    qi, kj = qi_of[step], kj_of[step]
    tq, tk = q_ref.shape[0], k_ref.shape[0]

    @pl.when(kj == 0)
    def _():
        _init_softmax_stats(m_sc, acc_sc)

    lane = lax.broadcasted_iota(jnp.int32, (1, LANES), 1)

    def all_heads(masked):
        def one_head(h, carry):
            c = pl.multiple_of(h * LANES, LANES)
            q, k, vt1 = q_ref[:, pl.ds(c, LANES)], k_ref[:, pl.ds(c, LANES)], _with_sum_rows(vt_ref[pl.ds(c, LANES), :])
            for mp in range(2):
                qm = jnp.where((lane < 64) == (mp == 0), q, jnp.zeros_like(q))
                st = lax.dot_general(k, qm, _NT, preferred_element_type=F32)
                if masked:
                    keep = (lax.broadcasted_iota(jnp.int32, (tk, tq), 0) <= lax.broadcasted_iota(jnp.int32, (tk, tq), 1))
                    st = jnp.where(keep, st, NEG_BIG)
                _softmax_step_t(st, vt1, 2 * h + mp, m_sc, acc_sc)
            return carry

        lax.fori_loop(0, DA_HEADS, one_head, 0)

    @pl.when(kj < qi)
    def _():
        all_heads(False)

    @pl.when(kj == qi)
    def _():
        all_heads(True)
        lp = lp_ref[...]
        lam = (jnp.exp(jnp.sum(lp[0:1] * lp[1:2], keepdims=True))
               - jnp.exp(jnp.sum(lp[2:3] * lp[3:4], keepdims=True)) + lam_init)

        def finish(h, carry):
            o = _softmax_result_t(acc_sc[2 * h]) - lam * _softmax_result_t(acc_sc[2 * h + 1])
            y = o * lax.rsqrt(jnp.mean(o * o, axis=0, keepdims=True) + NORM_EPS) * g_ref[...]
            o_ref[:, pl.ds(pl.multiple_of(h * LANES, LANES), LANES)] = (y * (1.0 - lam_init)).T.astype(o_ref.dtype)
            return carry

        lax.fori_loop(0, DA_HEADS, finish, 0)


def da_flash(q, k, vt, lam_params, subln_g, lam_init):
    t = q.shape[0]
    tile = _tile(t, ATTN_TILE)
    qi_of, kj_of = _causal_steps(t, tile)
    grid_spec = pltpu.PrefetchScalarGridSpec(
        num_scalar_prefetch=2,
        grid=(qi_of.shape[0],),
        in_specs=[pl.BlockSpec((tile, DA_WIDTH), lambda s, qi, kj: (qi[s], 0)),
                  pl.BlockSpec((tile, DA_WIDTH), lambda s, qi, kj: (kj[s], 0)),
                  pl.BlockSpec((DA_WIDTH, tile), lambda s, qi, kj: (0, kj[s])),
                  pl.BlockSpec((4, DA_DIM), lambda s, qi, kj: (0, 0)),
                  pl.BlockSpec((LANES, 1), lambda s, qi, kj: (0, 0))],
        out_specs=pl.BlockSpec((tile, DA_WIDTH), lambda s, qi, kj: (qi[s], 0)),
        scratch_shapes=[pltpu.VMEM((2 * DA_HEADS, 1, tile), F32),
                        pltpu.VMEM((2 * DA_HEADS, LANES + SUM_ROWS, tile), F32)])
    return pl.pallas_call(
        functools.partial(_da_flash_kernel, lam_init),
        out_shape=jax.ShapeDtypeStruct((t, DA_WIDTH), BF16),
        grid_spec=grid_spec,
        compiler_params=_cparams("arbitrary"),
    )(qi_of, kj_of, q, k, vt, lam_params, subln_g.reshape(LANES, 1))


GDN_GROUP = 256


def _softplus(x):
    return jnp.maximum(x, 0.0) + jnp.log1p(jnp.exp(-jnp.abs(x)))


def _gdn_gates_kernel(b_ref, a_ref, at_ref, alr_ref, dtr_ref, alc_ref, dtc_ref,
                      beta_ref, gcum_ref, grev_ref, gcumt_ref, egt_ref):
    n = b_ref.shape[0]
    r = lax.broadcasted_iota(jnp.int32, (n, n), 0)
    c = lax.broadcasted_iota(jnp.int32, (n, n), 1)
    same = (r // GDN_CHUNK) == (c // GDN_CHUNK)
    incl = jnp.where(same & (c <= r), 1.0, 0.0)
    rev = jnp.where(same & (c > r), 1.0, 0.0)
    beta_ref[...] = jax.nn.sigmoid(b_ref[...])
    g = -jnp.exp(alr_ref[...]) * _softplus(a_ref[...] + dtr_ref[...])
    gcum_ref[...] = jnp.dot(incl, g, precision=HIGHEST, preferred_element_type=F32)
    grev_ref[...] = jnp.dot(rev, g, precision=HIGHEST, preferred_element_type=F32)
    gt = -jnp.exp(alc_ref[...]) * _softplus(at_ref[...] + dtc_ref[...])
    gcumt = lax.dot_general(gt, incl, _NT, precision=HIGHEST, preferred_element_type=F32)
    gcumt_ref[...] = gcumt
    egt_ref[...] = jnp.exp(gcumt)


def gdn_gates(b_in, a_in, a_t, a_log, dt_bias):
    t, h = b_in.shape
    n = _tile(t, GDN_GROUP)
    colspec = pl.BlockSpec((n, h), lambda i: (i, 0))
    rowspec = pl.BlockSpec((h, n), lambda i: (0, i))
    vr = pl.BlockSpec((1, h), lambda i: (0, 0))
    vc = pl.BlockSpec((h, 1), lambda i: (0, 0))
    cs = jax.ShapeDtypeStruct((t, h), F32)
    rs = jax.ShapeDtypeStruct((h, t), F32)
    return pl.pallas_call(
        _gdn_gates_kernel,
        out_shape=(cs, cs, cs, rs, rs),
        grid=(t // n,),
        in_specs=[colspec, colspec, rowspec, vr, vr, vc, vc],
        out_specs=(colspec, colspec, colspec, rowspec, rowspec),
        compiler_params=_cparams("parallel"),
    )(b_in, a_in, a_t, a_log.reshape(1, h), dt_bias.reshape(1, h), a_log.reshape(h, 1), dt_bias.reshape(h, 1))


def _gdn_local_kernel(xq_ref, xk_ref, xv_ref, hq_ref, hk_ref, hv_ref, wq_ref, wk_ref, wv_ref,
                      beta_ref, gcum_ref, grev_ref, gcumt_ref,
                      u_ref, w_ref, qd_ref, kdt_ref, attn_ref):
    i, hg = pl.program_id(0), pl.program_id(1)
    n = xq_ref.shape[0]
    nh = xq_ref.shape[1] // LANES

    def conv_silu(x_ref, halo_ref, cw_ref, hl):
        x, cw = x_ref[:, hl], cw_ref[:, hl]
        halo = jnp.where(i > 0, halo_ref[:, hl], 0.0)

        def taps(z):
            return (cw[3:4] * z + cw[2:3] * pltpu.roll(z, 1, 0) + cw[1:2] * pltpu.roll(z, 2, 0)
                    + cw[0:1] * pltpu.roll(z, 3, 0))

        top = taps(jnp.concatenate([halo, x[:8]], axis=0))[8:16]
        y = jnp.concatenate([top, taps(x)[8:]], axis=0)
        return y * jax.nn.sigmoid(y)

    def l2n(y):
        return y * lax.rsqrt(jnp.sum(y * y, axis=-1, keepdims=True) + NORM_EPS)

    r = lax.broadcasted_iota(jnp.int32, (n, n), 0)
    c = lax.broadcasted_iota(jnp.int32, (n, n), 1)
    same = (r // GDN_CHUNK) == (c // GDN_CHUNK)
    heads = range(nh)
    hls = [slice(hb * LANES, (hb + 1) * LANES) for hb in heads]
    pws, tinvs, rest = [], [], []
    for hb in heads:
        hl, h = hls[hb], hg * nh + hb
        q = l2n(conv_silu(xq_ref, hq_ref, wq_ref, hl)) * (GDN_DK ** -0.5)
        k = l2n(conv_silu(xk_ref, hk_ref, wk_ref, hl))
        v = conv_silu(xv_ref, hv_ref, wv_ref, hl)
        head = lax.broadcasted_iota(jnp.int32, (1, GDN_HEADS), 1) == h
        pick = lambda ref: jnp.sum(jnp.where(head, ref[...], 0.0), axis=-1, keepdims=True)
        beta, gc, grev = pick(beta_ref), pick(gcum_ref), pick(grev_ref)
        gr = gcumt_ref[pl.ds(h, 1), :]
        decay = jnp.exp(jnp.where(same & (c <= r), gc - gr, NEG_BIG))
        kb = k * beta
        k16 = k.astype(BF16)
        kk = lax.dot_general(kb.astype(BF16), k16, _NT, preferred_element_type=F32)
        qk = lax.dot_general(q.astype(BF16), k16, _NT, preferred_element_type=F32)
        pw = -jnp.where(same & (c < r), kk * decay, 0.0)
        pws.append(pw)
        tinvs.append(jnp.where(r == c, 1.0, 0.0) + pw)
        attn = qk * decay
        qd_ref[:, hl] = (q * jnp.exp(gc)).astype(BF16)
        kdt_ref[hl, :] = (k * jnp.exp(grev)).T.astype(BF16)
        left = attn[:, :LANES]
        for j in range(1, n // LANES):
            left = left + attn[:, j * LANES:(j + 1) * LANES]
        attn_ref[:, hl] = (left + pltpu.roll(left, 64, 1)).astype(BF16)
        rest.append(((v * beta).astype(BF16), (kb * jnp.exp(gc)).astype(BF16)))

    for _ in range(5):
        for hb in heads:
            p16 = pws[hb].astype(BF16)
            pws[hb] = jnp.dot(p16, p16, preferred_element_type=F32)
            tinvs[hb] = tinvs[hb] + jnp.dot(tinvs[hb].astype(BF16), pws[hb].astype(BF16), preferred_element_type=F32)
    for hb in heads:
        t16 = tinvs[hb].astype(BF16)
        u_ref[:, hls[hb]] = jnp.dot(t16, rest[hb][0], preferred_element_type=F32)
        w_ref[:, hls[hb]] = jnp.dot(t16, rest[hb][1], preferred_element_type=F32).astype(BF16)


GDN_LOCAL_HEADS = 4


def gdn_local(proj, qblk, conv_w, beta, gcum, grev, gcumt):
    t = proj.shape[0]
    n = _tile(t, GDN_GROUP)
    nh, hl = GDN_HEADS, GDN_LOCAL_HEADS
    wide = hl * LANES
    assert qblk % hl == 0 and nh % hl == 0
    x = lambda part: pl.BlockSpec((n, wide), lambda i, h, part=part: (i, (qblk + part * nh) // hl + h))
    halo = lambda part: pl.BlockSpec(
        (8, wide), lambda i, h, part=part: (jnp.maximum(i * (n // 8) - 1, 0), (qblk + part * nh) // hl + h))
    cw = lambda part: pl.BlockSpec((GDN_CONV, wide), lambda i, h, part=part: (0, part * nh // hl + h))
    col = pl.BlockSpec((n, nh), lambda i, h: (i, 0))
    row = pl.BlockSpec((nh, n), lambda i, h: (0, i))
    out = pl.BlockSpec((n, wide), lambda i, h: (i, h))
    f32o = jax.ShapeDtypeStruct((t, GDN_WIDTH), F32)
    b16o = jax.ShapeDtypeStruct((t, GDN_WIDTH), BF16)
    return pl.pallas_call(
        _gdn_local_kernel,
        out_shape=(f32o, b16o, b16o, jax.ShapeDtypeStruct((GDN_WIDTH, t), BF16), b16o),
        grid=(t // n, nh // hl),
        in_specs=[x(0), x(1), x(2), halo(0), halo(1), halo(2), cw(0), cw(1), cw(2), col, col, col, row],
        out_specs=(out, out, out, pl.BlockSpec((wide, n), lambda i, h: (h, i)), out),
        compiler_params=_cparams("parallel", "parallel"),
    )(proj, proj, proj, proj, proj, proj, conv_w, conv_w, conv_w, beta, gcum, grev, gcumt)


GDN_SCAN_HEADS = 8


def _gdn_scan_kernel(egl_ref, u_ref, w_ref, qd_ref, kdt_ref, attn_ref, z_ref, g_ref, o_ref, s_ref):
    hg, i = pl.program_id(0), pl.program_id(1)
    n = u_ref.shape[0]
    cpg = n // GDN_CHUNK
    nh = s_ref.shape[0]

    @pl.when(i == 0)
    def _():
        s_ref[...] = jnp.zeros(s_ref.shape, F32)

    states = [s_ref[hb] for hb in range(nh)]
    zeros = jnp.zeros((GDN_CHUNK, GDN_DV), BF16)
    for c in range(cpg):
        sl = slice(c * GDN_CHUNK, (c + 1) * GDN_CHUNK)
        pair = slice((c // 2) * LANES, (c // 2 + 1) * LANES)
        for hb in range(nh):
            hl = slice(hb * LANES, (hb + 1) * LANES)
            s16 = states[hb].astype(BF16)
            vnew = u_ref[sl, hl] - jnp.dot(w_ref[sl, hl], s16, preferred_element_type=F32)
            v16 = vnew.astype(BF16)
            o = (jnp.dot(qd_ref[sl, hl], s16, preferred_element_type=F32)
                 + jnp.dot(attn_ref[sl, hb * LANES:hb * LANES + GDN_CHUNK], v16, preferred_element_type=F32))
            vpad = jnp.concatenate([v16, zeros] if c % 2 == 0 else [zeros, v16], axis=0)
            states[hb] = (states[hb] * egl_ref[hg * nh + hb, i * cpg + c]
                          + jnp.dot(kdt_ref[hl, pair], vpad, preferred_element_type=F32))
            y = o * lax.rsqrt(jnp.mean(o * o, axis=-1, keepdims=True) + NORM_EPS) * g_ref[...]
            z = z_ref[sl, hl]
            o_ref[sl, hl] = (y * (z * jax.nn.sigmoid(z))).astype(o_ref.dtype)
    for hb in range(nh):
        s_ref[hb] = states[hb]


def gdn_scan(egl, u, w, qd, kdt, attn, proj, zblk, norm_g):
    t = u.shape[0]
    n = _tile(t, GDN_GROUP)
    nh = GDN_SCAN_HEADS
    wide = nh * LANES
    assert zblk % nh == 0 and GDN_HEADS % nh == 0
    blk = pl.BlockSpec((n, wide), lambda h, i: (i, h))
    return pl.pallas_call(
        _gdn_scan_kernel,
        out_shape=jax.ShapeDtypeStruct((t, GDN_WIDTH), BF16),
        grid=(GDN_HEADS // nh, t // n),
        in_specs=[pl.BlockSpec(memory_space=pltpu.SMEM), blk, blk, blk,
                  pl.BlockSpec((wide, n), lambda h, i: (h, i)), blk,
                  pl.BlockSpec((n, wide), lambda h, i: (i, zblk // nh + h)),
                  pl.BlockSpec((1, LANES), lambda h, i: (0, 0))],
        out_specs=blk,
        scratch_shapes=[pltpu.VMEM((nh, GDN_DK, GDN_DV), F32)],
        compiler_params=_cparams("parallel", "arbitrary"),
    )(egl, u, w, qd, kdt, attn, proj, norm_g.reshape(1, LANES))


def gated_deltanet(proj, small_b, small_a, qblk, zblk, conv_w, a_log, dt_bias, norm_g):
    beta, gcum, grev, gcumt, egt = gdn_gates(small_b, small_a, small_a.T, a_log, dt_bias)
    u, w, qd, kdt, attn = gdn_local(proj, qblk, conv_w, beta, gcum, grev, gcumt)
    egl = egt[:, GDN_CHUNK - 1::GDN_CHUNK]
    return gdn_scan(egl, u, w, qd, kdt, attn, proj, zblk, norm_g)


DSA_SEL_TQ = 128
DSA_SEL_TK = 512


def _dsa_prep_kernel(q_ref, k_ref, v_ref, qi_ref, ki_ref, wi_ref, cos64_ref, sin64_ref, cos128_ref, sin128_ref,
                     qg_ref, kg_ref, qo_ref, ko_ref, vo_ref, qio_ref, kio_ref, wio_ref):
    lane = lax.broadcasted_iota(jnp.int32, (1, LANES), 1)
    cos64, sin64 = cos64_ref[...], sin64_ref[...]
    cos128, sin128 = cos128_ref[...], sin128_ref[...]
    for c in range(DSA_HEADS):
        sl = slice(c * LANES, (c + 1) * LANES)
        for x_ref, g_ref, o_ref, scale in ((q_ref, qg_ref, qo_ref, DSA_DIM ** -0.5 * LOG2E), (k_ref, kg_ref, ko_ref, 1.0)):
            x = x_ref[:, sl]
            y = x * lax.rsqrt(jnp.mean(x * x, axis=-1, keepdims=True) + NORM_EPS) * g_ref[...]
            o_ref[:, sl] = (_rope128(y, cos128, sin128) * scale).astype(BF16)
    vo_ref[...] = v_ref[...].T.astype(BF16)
    tq = DSA_SEL_TQ
    for c in range(IDX_HEADS * IDX_DIM // LANES):
        y = _rope64(qi_ref[:, c * LANES:(c + 1) * LANES], cos64, sin64, lane).astype(BF16)
        for b in range(qi_ref.shape[0] // tq):
            for half in range(2):
                h = 2 * c + half
                qio_ref[b, h * tq:(h + 1) * tq, :] = y[b * tq:(b + 1) * tq, half * IDX_DIM:(half + 1) * IDX_DIM]
    kio_ref[...] = _rope64(ki_ref[...], cos64, sin64, lane).astype(BF16)
    wio_ref[...] = wi_ref[...] * ((IDX_HEADS * IDX_DIM) ** -0.5)


def dsa_prep(proj, qblk, qiblk, ki, wi, tabs64, tabs128, qg, kg):
    t = proj.shape[0]
    tm = _tile(t, 256)
    w = DSA_WIDTH
    nb = tm // DSA_SEL_TQ
    col = lambda c: pl.BlockSpec((tm, w), lambda i, c=c: (i, c))
    row = pl.BlockSpec((tm, LANES), lambda i: (i, 0))
    vec = pl.BlockSpec((1, LANES), lambda i: (0, 0))
    out = pl.BlockSpec((tm, w), lambda i: (i, 0))
    shp = jax.ShapeDtypeStruct((t, w), BF16)
    return pl.pallas_call(
        _dsa_prep_kernel,
        out_shape=(shp, shp, jax.ShapeDtypeStruct((w, t), BF16),
                   jax.ShapeDtypeStruct((t // DSA_SEL_TQ, IDX_HEADS * DSA_SEL_TQ, IDX_DIM), BF16),
                   jax.ShapeDtypeStruct((t, LANES), BF16), jax.ShapeDtypeStruct((t, IDX_HEADS), F32)),
        grid=(t // tm,),
        in_specs=[col(qblk), col(qblk + 1), col(qblk + 2),
                  pl.BlockSpec((tm, IDX_HEADS * IDX_DIM), lambda i: (i, qiblk)), row,
                  pl.BlockSpec((tm, IDX_HEADS), lambda i: (i, 0)), row, row, row, row, vec, vec],
        out_specs=(out, out, pl.BlockSpec((w, tm), lambda i: (0, i)),
                   pl.BlockSpec((nb, IDX_HEADS * DSA_SEL_TQ, IDX_DIM), lambda i: (i, 0, 0)), row,
                   pl.BlockSpec((tm, IDX_HEADS), lambda i: (i, 0))),
        compiler_params=_cparams("parallel"),
    )(proj, proj, proj, proj, ki, wi, *tabs64, *tabs128, qg.reshape(1, LANES), kg.reshape(1, LANES))


def _order_key(x):
    bits = pltpu.bitcast(x, jnp.int32)
    return jnp.where(bits < 0, bits ^ 0x7FFFFFFF, bits)


def _order_value(key):
    return pltpu.bitcast(jnp.where(key < 0, key ^ 0x7FFFFFFF, key), F32)


def _dsa_select_kernel(n_sel, qi_ref, ki_ref, wt_ref, mask_ref, keys_sc):
    i = pl.program_id(0)
    tq, tk = DSA_SEL_TQ, DSA_SEL_TK
    t = ki_ref.shape[0]
    nvalid = ((i + 1) * tq + tk - 1) // tk
    qpos = i * tq + lax.broadcasted_iota(jnp.int32, (1, tq), 1)
    w = wt_ref[...]
    q_all = qi_ref[0]
    chunk = lambda c: pl.ds(pl.multiple_of(c * tk, tk), tk)

    def score_chunk(c, carry):
        lg = lax.dot_general(ki_ref[chunk(c), :IDX_DIM], q_all, _NT, preferred_element_type=F32)
        acc = jnp.zeros((tk, tq), F32)
        for h in range(IDX_HEADS):
            acc = acc + w[h:h + 1, :] * jnp.maximum(lg[:, h * tq:(h + 1) * tq], 0.0)
        key = _order_key(acc)
        kpos = c * tk + lax.broadcasted_iota(jnp.int32, (tk, 1), 0)
        causal = kpos <= qpos
        keys_sc[chunk(c), :] = jnp.where(causal, key, INT_MIN)
        kmin8, kmax8, sum8, sq8 = carry
        fold = lambda x: x.reshape(tk // 8, 8, tq)
        live = jnp.where(causal, acc, 0.0)
        return (jnp.minimum(kmin8, jnp.min(fold(jnp.where(causal, key, INT_MAX)), axis=0)),
                jnp.maximum(kmax8, jnp.max(fold(jnp.where(causal, key, INT_MIN)), axis=0)),
                sum8 + jnp.sum(fold(live), axis=0), sq8 + jnp.sum(fold(live * live), axis=0))

    zeros8 = jnp.zeros((8, tq), F32)
    kmin8, kmax8, sum8, sq8 = lax.fori_loop(0, nvalid, score_chunk, (
        jnp.full((8, tq), INT_MAX, jnp.int32), jnp.full((8, tq), INT_MIN, jnp.int32), zeros8, zeros8))

    def count_ge(thr):
        def body(c, acc):
            hit = jnp.where(keys_sc[chunk(c), :] >= thr, 1, 0)
            return acc + jnp.sum(hit.reshape(tk // 8, 8, tq), axis=0)

        acc = lax.fori_loop(0, nvalid, body, jnp.zeros((8, tq), jnp.int32))
        return jnp.sum(acc, axis=0, keepdims=True)

    lo0 = jnp.min(kmin8, axis=0, keepdims=True)
    hi0 = jnp.max(kmax8, axis=0, keepdims=True) + 1
    cnt_lo0 = qpos + 1
    cnt_hi0 = jnp.zeros((1, tq), jnp.int32)

    def unsettled(lo, hi, cnt_lo):
        return jnp.max(jnp.where((cnt_lo > n_sel) & (lo + 1 != hi), 1, 0))

    n_live = cnt_lo0.astype(F32)
    mean = jnp.sum(sum8, axis=0, keepdims=True) / n_live
    dev = jnp.sqrt(jnp.maximum(jnp.sum(sq8, axis=0, keepdims=True) / n_live - mean * mean, 0.0))
    tail = jnp.clip(n_sel / n_live, 1e-6, 0.5)
    tt = jnp.sqrt(-2.0 * jnp.log(tail))
    z = tt - (2.30753 + 0.27061 * tt) / (1.0 + 0.99229 * tt + 0.04481 * tt * tt)
    slope = dev / (z + 1.0 / jnp.maximum(z, 0.5))
    log_n = math.log(n_sel)

    def narrow(carry):
        lo, hi, cnt_lo, cnt_hi, w_lo, w_hi, last_ok, it, _ = carry
        half = (lo >> 1) + (hi >> 1) + (lo & hi & 1)
        v_lo, v_hi = _order_value(lo), _order_value(hi)
        e_lo, e_hi = jnp.log(cnt_lo.astype(F32) + 0.5) - log_n, jnp.log(cnt_hi.astype(F32) + 0.5) - log_n
        f_lo, f_hi = e_lo * w_lo, e_hi * w_hi
        false_pos = v_lo + f_lo / (f_lo - f_hi) * (v_hi - v_lo)
        along_tail = jnp.where(last_ok > 0, v_lo + e_lo * slope, v_hi + e_hi * slope)
        guess = _order_key(jnp.where(it == 0, mean + z * dev, jnp.where(it == 1, along_tail, false_pos)))
        mid = jnp.where((guess > lo) & (guess < hi) & ((it & 7) != 7), guess, half)
        cnt = count_ge(mid)
        ok = cnt >= n_sel
        w_hi = jnp.where(ok, jnp.where(last_ok > 0, 0.5 * w_hi, w_hi), 1.0)
        w_lo = jnp.where(ok, 1.0, jnp.where(last_ok < 0, 0.5 * w_lo, w_lo))
        lo, cnt_lo = jnp.where(ok, mid, lo), jnp.where(ok, cnt, cnt_lo)
        hi, cnt_hi = jnp.where(ok, hi, mid), jnp.where(ok, cnt_hi, cnt)
        return lo, hi, cnt_lo, cnt_hi, w_lo, w_hi, jnp.where(ok, 1, -1), it + 1, unsettled(lo, hi, cnt_lo)

    ones = jnp.ones((1, tq), F32)
    lo = lax.while_loop(lambda cr: cr[8] > 0, narrow,
                        (lo0, hi0, cnt_lo0, cnt_hi0, ones, ones, jnp.zeros((1, tq), jnp.int32), jnp.int32(0),
                         unsettled(lo0, hi0, cnt_lo0)))[0]

    def write_valid(c, carry):
        mask_ref[chunk(c), :] = jnp.where(keys_sc[chunk(c), :] >= lo, 1.0, 0.0).astype(mask_ref.dtype)
        return carry

    def write_zero(c, carry):
        mask_ref[chunk(c), :] = jnp.zeros((tk, tq), mask_ref.dtype)
        return carry

    lax.fori_loop(0, nvalid, write_valid, 0)
    lax.fori_loop(nvalid, t // tk, write_zero, 0)


def dsa_select(qi, ki, wt, n_sel):
    t = ki.shape[0]
    tq = DSA_SEL_TQ
    return pl.pallas_call(
        functools.partial(_dsa_select_kernel, n_sel),
        out_shape=jax.ShapeDtypeStruct((t, t), BF16),
        grid=(t // tq,),
        in_specs=[pl.BlockSpec((1, IDX_HEADS * tq, IDX_DIM), lambda i: (i, 0, 0)),
                  pl.BlockSpec((t, LANES), lambda i: (0, 0)),
                  pl.BlockSpec((IDX_HEADS, tq), lambda i: (0, i))],
        out_specs=pl.BlockSpec((t, tq), lambda i: (0, i)),
        scratch_shapes=[pltpu.VMEM((t, tq), jnp.int32)],
        compiler_params=_cparams("parallel"),
    )(qi, ki, wt)


def _dsa_flash_kernel(qi_of, kj_of, q_ref, k_ref, vt_ref, mask_ref, o_ref, m_sc, acc_sc):
    step = pl.program_id(0)
    qi, kj = qi_of[step], kj_of[step]

    @pl.when(kj == 0)
    def _():
        _init_softmax_stats(m_sc, acc_sc)

    def one_head(h, carry):
        c = pl.ds(pl.multiple_of(h * LANES, LANES), LANES)
        st = lax.dot_general(k_ref[:, c], q_ref[:, c], _NT, preferred_element_type=F32)
        st = jnp.where(mask_ref[...] > 0, st, NEG_BIG)
        _softmax_step_t(st, _with_sum_rows(vt_ref[c, :]), h, m_sc, acc_sc)
        return carry

    lax.fori_loop(0, DSA_HEADS, one_head, 0)

    @pl.when(kj == qi)
    def _():
        def finish(h, carry):
            o_ref[:, pl.ds(pl.multiple_of(h * LANES, LANES), LANES)] = _softmax_result_t(acc_sc[h]).T.astype(o_ref.dtype)
            return carry

        lax.fori_loop(0, DSA_HEADS, finish, 0)


def dsa_flash(q, k, vt, mask_t):
    t = q.shape[0]
    tile = _tile(t, ATTN_TILE)
    qi_of, kj_of = _causal_steps(t, tile)
    grid_spec = pltpu.PrefetchScalarGridSpec(
        num_scalar_prefetch=2,
        grid=(qi_of.shape[0],),
        in_specs=[pl.BlockSpec((tile, DSA_WIDTH), lambda s, qi, kj: (qi[s], 0)),
                  pl.BlockSpec((tile, DSA_WIDTH), lambda s, qi, kj: (kj[s], 0)),
                  pl.BlockSpec((DSA_WIDTH, tile), lambda s, qi, kj: (0, kj[s])),
                  pl.BlockSpec((tile, tile), lambda s, qi, kj: (kj[s], qi[s]))],
        out_specs=pl.BlockSpec((tile, DSA_WIDTH), lambda s, qi, kj: (qi[s], 0)),
        scratch_shapes=[pltpu.VMEM((DSA_HEADS, 1, tile), F32),
                        pltpu.VMEM((DSA_HEADS, LANES + SUM_ROWS, tile), F32)])
    return pl.pallas_call(
        _dsa_flash_kernel,
        out_shape=jax.ShapeDtypeStruct((t, DSA_WIDTH), BF16),
        grid_spec=grid_spec,
        compiler_params=_cparams("arbitrary"),
    )(qi_of, kj_of, q, k, vt, mask_t)


def dsa_attention(proj, qblk, qiblk, ki, wi, tabs64, tabs128, qg, kg):
    t = proj.shape[0]
    q, k, vt, qi, kir, wis = dsa_prep(proj, qblk, qiblk, ki, wi, tabs64, tabs128, qg, kg)
    mask_t = dsa_select(qi, kir, wis.T, min(TOPK_MAX, t // 4))
    return dsa_flash(q, k, vt, mask_t)


def _take_top(s, count):
    n = s.shape[0]
    idx = lax.broadcasted_iota(jnp.int32, s.shape, 0)
    vals = []
    for _ in range(count):
        m = jnp.max(s, axis=0, keepdims=True)
        first = jnp.min(jnp.where(s == m, idx, n), axis=0, keepdims=True)
        s = jnp.where(idx == first, NEG_BIG, s)
        vals.append(m)
    return vals


def _peer_route_kernel(q_ref, sk_ref, s1_ref, s2_ref, e1_ref, e2_ref, tau_ref):
    for h in range(PEER_HEADS):
        halves = []
        for c in range(2):
            sl = slice((2 * h + c) * LANES, (2 * h + c + 1) * LANES)
            halves.append(lax.dot_general(sk_ref[h, c], q_ref[:, sl].astype(BF16), _NT, preferred_element_type=F32))
        s1, s2 = halves
        v1, v2 = _take_top(s1, PEER_TOPK), _take_top(s2, PEER_TOPK)
        v1_all, v2_all = jnp.concatenate(v1, axis=0), jnp.concatenate(v2, axis=0)
        cand = jnp.concatenate([v1[0] + v2_all] + [v1[a] + v2_all[:8] for a in range(1, 8)] + [v1_all[8:] + v2[0]],
                               axis=0)
        top = _take_top(cand, PEER_TOPK)
        z = sum(jnp.exp(tv - top[0]) for tv in top)
        s1_ref[h], s2_ref[h] = s1, s2
        e1_ref[h] = jnp.exp(s1 - v1[0])
        e2_ref[h] = jnp.exp(s2 - v2[0]) / z
        tau_ref[h:h + 1, :] = top[-1]


def peer_route(q, sub_keys):
    t = q.shape[0]
    tm = _tile(t, 256)
    big = pl.BlockSpec((PEER_HEADS, PEER_NKEYS, tm), lambda i: (0, 0, i))
    bs = jax.ShapeDtypeStruct((PEER_HEADS, PEER_NKEYS, t), F32)
    return pl.pallas_call(
        _peer_route_kernel,
        out_shape=(bs, bs, bs, bs, jax.ShapeDtypeStruct((PEER_HEADS, t), F32)),
        grid=(t // tm,),
        in_specs=[pl.BlockSpec((tm, PEER_HEADS * PEER_DKEY), lambda i: (i, 0)),
                  pl.BlockSpec(sub_keys.shape, lambda i: (0, 0, 0, 0))],
        out_specs=(big, big, big, big, pl.BlockSpec((PEER_HEADS, tm), lambda i: (0, i))),
        compiler_params=_cparams("parallel"),
    )(q, sub_keys)


def _gelu_tanh(x):
    return 0.5 * x * (1.0 + jnp.tanh(math.sqrt(2.0 / math.pi) * (x + 0.044715 * (x * x * x))))


def _peer_expert_kernel(x_ref, u_ref, v_ref, s1_ref, s2_ref, e1_ref, e2_ref, tau_ref, o_ref):
    j = pl.program_id(1)
    te = u_ref.shape[0]

    @pl.when(j == 0)
    def _():
        o_ref[...] = jnp.zeros(o_ref.shape, F32)

    act = _gelu_tanh(lax.dot_general(u_ref[...], x_ref[...], _NT, preferred_element_type=F32))
    rows_per = te // PEER_NKEYS
    gates = []
    for a in range(rows_per):
        n1 = j * rows_per + a
        g = None
        for h in range(PEER_HEADS):
            pair = s1_ref[h, pl.ds(n1, 1), :] + s2_ref[h]
            gh = jnp.where(pair >= tau_ref[h:h + 1, :], e1_ref[h, pl.ds(n1, 1), :] * e2_ref[h], 0.0)
            g = gh if g is None else g + gh
        gates.append(g)
    ga = (jnp.concatenate(gates, axis=0) * act).astype(BF16)
    o_ref[...] += lax.dot_general(ga, v_ref[...], (((0,), (0,)), ((), ())), preferred_element_type=F32)


def peer_expert(x, u, v, s1, s2, e1, e2, tau, tm=512, te=512):
    t, d = x.shape
    e = u.shape[0]
    tm, te = _tile(t, tm), _tile(e, te)
    once = pl.Buffered(1)
    big = pl.BlockSpec((PEER_HEADS, PEER_NKEYS, tm), lambda i, j: (0, 0, i), pipeline_mode=once)
    return pl.pallas_call(
        _peer_expert_kernel,
        out_shape=jax.ShapeDtypeStruct((t, d), F32),
        grid=(t // tm, e // te),
        in_specs=[pl.BlockSpec((tm, d), lambda i, j: (i, 0), pipeline_mode=once),
                  pl.BlockSpec((te, d), lambda i, j: (j, 0)),
                  pl.BlockSpec((te, d), lambda i, j: (j, 0)),
                  big, big, big, big,
                  pl.BlockSpec((PEER_HEADS, tm), lambda i, j: (0, i), pipeline_mode=once)],
        out_specs=pl.BlockSpec((tm, d), lambda i, j: (i, 0)),
        compiler_params=_cparams("parallel", "arbitrary"),
    )(x, u, v, s1, s2, e1, e2, tau)


def _add_rmsnorm_kernel(h_ref, y_ref, g_ref, ho_ref, ao_ref):
    x = h_ref[...] + y_ref[...]
    ho_ref[...] = x
    ao_ref[...] = (x * lax.rsqrt(jnp.mean(x * x, axis=-1, keepdims=True) + NORM_EPS) * g_ref[...]).astype(BF16)


def add_rmsnorm(h, y, g):
    t, d = h.shape
    tm = _tile(t, 256)
    blk = pl.BlockSpec((tm, d), lambda i: (i, 0))
    return pl.pallas_call(
        _add_rmsnorm_kernel,
        out_shape=(jax.ShapeDtypeStruct((t, d), F32), jax.ShapeDtypeStruct((t, d), BF16)),
        grid=(t // tm,),
        in_specs=[blk, blk, pl.BlockSpec((1, d), lambda i: (0, 0))],
        out_specs=(blk, blk),
        compiler_params=_cparams("parallel"),
    )(h, y, g.reshape(1, d))


def _ple_kernel(a_ref, wg_ref, p_ref, wp_ref, h_ref, o_ref):
    gate = jax.nn.sigmoid(jnp.dot(a_ref[...], wg_ref[...], preferred_element_type=F32))
    o_ref[...] = h_ref[...] + gate * jnp.dot(p_ref[...], wp_ref[...], preferred_element_type=F32)


def ple(a, wg, p, wp, h, tm=512, tn=1024):
    m, k = a.shape
    n = wg.shape[1]
    kp = p.shape[1]
    tm, tn = _tile(m, tm), _tile(n, tn)
    return pl.pallas_call(
        _ple_kernel,
        out_shape=jax.ShapeDtypeStruct((m, n), F32),
        grid=(n // tn, m // tm),
        in_specs=[pl.BlockSpec((tm, k), lambda j, i: (i, 0)), pl.BlockSpec((k, tn), lambda j, i: (0, j)),
                  pl.BlockSpec((tm, kp), lambda j, i: (i, 0)), pl.BlockSpec((kp, tn), lambda j, i: (0, j)),
                  pl.BlockSpec((tm, tn), lambda j, i: (i, j))],
        out_specs=pl.BlockSpec((tm, tn), lambda j, i: (i, j)),
        compiler_params=_cparams("parallel", "parallel"),
    )(a, wg, p, wp, h)


_QBLK_GDN, _ZBLK_GDN = 0, 48
_BLK_DA = 8
_BLK_DSA = 11
_BLK_IDX = 7


def _regroup_w_in(w):
    sizes = (DA_WIDTH, DA_WIDTH, DA_WIDTH, 3 * GDN_WIDTH, GDN_WIDTH, GDN_HEADS, GDN_HEADS,
             DSA_WIDTH, DSA_WIDTH, DSA_WIDTH, IDX_HEADS * IDX_DIM, IDX_DIM, IDX_HEADS)
    offs = [0]
    for s in sizes:
        offs.append(offs[-1] + s)
    (da_q, da_k, da_v, g_qkv, g_z, g_b, g_a, c_q, c_k, c_v, c_qi, c_ki, c_w) = [
        w[:, offs[n]:offs[n + 1]] for n in range(len(sizes))]
    main = jnp.concatenate([g_qkv, g_z, da_q, da_k, da_v, c_q, c_k, c_v, c_qi], axis=1).astype(BF16)
    small = jnp.concatenate([c_ki, c_w, g_b, g_a], axis=1).astype(BF16)
    return main, small


def kernel(x, p, attn_norm, w_in, da_q_norm, da_k_norm, da_lambda, da_subln, gdn_conv, gdn_a_log, gdn_dt_bias, gdn_norm, dsa_q_norm, dsa_k_norm, w_out, ffn_norm, peer_w_q, peer_sub_keys, peer_u, peer_v, ple_norm, w_ple_gate, w_ple_proj):
    b, t, d = x.shape
    assert b == 1
    depth = w_in.shape[0]
    tabs64, tabs128 = _rope_tables(t, 64), _rope_tables(t, 128)
    h = x.reshape(t, d)
    for i in range(depth):
        w_main, w_small = _regroup_w_in(w_in[i])
        a = rmsnorm_bf16(h, attn_norm[i])
        proj = matmul(a, w_main)
        small = matmul(a, w_small)
        lam_init = 0.8 - 0.6 * math.exp(-0.3 * i)
        qa, ka, va = da_prep(proj, _BLK_DA, *tabs64, da_q_norm[i], da_k_norm[i])
        o_a = da_flash(qa, ka, va, da_lambda[i], da_subln[i], lam_init)
        o_b = gated_deltanet(proj, small[:, 96:112], small[:, 112:128], _QBLK_GDN, _ZBLK_GDN, gdn_conv[i],
                             gdn_a_log[i], gdn_dt_bias[i], gdn_norm[i])
        o_c = dsa_attention(proj, _BLK_DSA, _BLK_IDX, small, small[:, 64:96], tabs64, tabs128,
                            dsa_q_norm[i], dsa_k_norm[i])
        mix = jnp.concatenate([o_a, o_b, o_c], axis=-1)
        h = matmul(mix, w_out[i].astype(BF16), residual=h)

        a = rmsnorm_bf16(h, ffn_norm[i])
        pq = matmul(a, peer_w_q[i].astype(BF16))
        s1, s2, e1, e2, tau = peer_route(pq, peer_sub_keys[i].astype(BF16))
        y = peer_expert(a, peer_u[i].astype(BF16), peer_v[i].astype(BF16), s1, s2, e1, e2, tau)
        h, a = add_rmsnorm(h, y, ple_norm[i])
        h = ple(a, w_ple_gate[i].astype(BF16), p[i, 0].astype(BF16), w_ple_proj[i].astype(BF16), h)
    return h.reshape(b, t, d)
```

```python
import functools
import math

import jax
import jax.numpy as jnp
from jax import lax
from jax.experimental import pallas as pl
from jax.experimental.pallas import tpu as pltpu

F32 = jnp.float32
BF16 = jnp.bfloat16
HIGHEST = lax.Precision.HIGHEST

NORM_EPS = 1e-6
ROPE_THETA = 10000.0
LANES = 128
VMEM_LIMIT = 56 * 1024 * 1024

DA_HEADS, DA_DIM = 8, 64
DA_WIDTH = DA_HEADS * 2 * DA_DIM
GDN_HEADS, GDN_DK, GDN_DV, GDN_CONV, GDN_CHUNK = 16, 128, 128, 4, 64
GDN_WIDTH = GDN_HEADS * GDN_DV
DSA_HEADS, DSA_DIM = 8, 128
DSA_WIDTH = DSA_HEADS * DSA_DIM
IDX_HEADS, IDX_DIM = 32, 64
TOPK_MAX = 256
PEER_HEADS, PEER_NKEYS, PEER_DKEY, PEER_TOPK = 8, 128, 256, 16
PLE_DIM = 256

LOG2E = math.log2(math.e)
NEG_BIG = -1e30
INT_MIN = -(2 ** 31)
INT_MAX = 2 ** 31 - 1

_NT = (((1,), (1,)), ((), ()))


def _cparams(*sem):
    return pltpu.CompilerParams(dimension_semantics=sem, vmem_limit_bytes=VMEM_LIMIT)


def _tile(n, pref):
    t = min(n, pref)
    assert n % t == 0, (n, pref)
    return t


def _rmsnorm_kernel(x_ref, g_ref, o_ref):
    x = x_ref[...]
    y = x * lax.rsqrt(jnp.mean(x * x, axis=-1, keepdims=True) + NORM_EPS) * g_ref[...]
    o_ref[...] = y.astype(o_ref.dtype)


def rmsnorm_bf16(x, g):
    t, d = x.shape
    tm = _tile(t, 256)
    return pl.pallas_call(
        _rmsnorm_kernel,
        out_shape=jax.ShapeDtypeStruct((t, d), BF16),
        grid=(t // tm,),
        in_specs=[pl.BlockSpec((tm, d), lambda i: (i, 0)), pl.BlockSpec((1, d), lambda i: (0, 0))],
        out_specs=pl.BlockSpec((tm, d), lambda i: (i, 0)),
        compiler_params=_cparams("parallel"),
    )(x, g.reshape(1, d))


def _matmul_kernel(a_ref, b_ref, o_ref):
    o_ref[...] = jnp.dot(a_ref[...], b_ref[...], preferred_element_type=F32)


def _matmul_res_kernel(a_ref, b_ref, r_ref, o_ref):
    o_ref[...] = r_ref[...] + jnp.dot(a_ref[...], b_ref[...], preferred_element_type=F32)


def matmul(a, b, residual=None, tm=512, tn=1024):
    m, k = a.shape
    n = b.shape[1]
    tm, tn = _tile(m, tm), _tile(n, tn)
    in_specs = [pl.BlockSpec((tm, k), lambda j, i: (i, 0)), pl.BlockSpec((k, tn), lambda j, i: (0, j))]
    args = [a, b]
    body = _matmul_kernel
    if residual is not None:
        in_specs.append(pl.BlockSpec((tm, tn), lambda j, i: (i, j)))
        args.append(residual)
        body = _matmul_res_kernel
    return pl.pallas_call(
        body,
        out_shape=jax.ShapeDtypeStruct((m, n), F32),
        grid=(n // tn, m // tm),
        in_specs=in_specs,
        out_specs=pl.BlockSpec((tm, tn), lambda j, i: (i, j)),
        compiler_params=_cparams("parallel", "parallel"),
    )(*args)


def _rope_tables(t, d):
    pos = jnp.arange(t, dtype=F32)
    inv = ROPE_THETA ** (-jnp.arange(0, d, 2, dtype=F32) / d)
    ang = pos[:, None] * inv[None, :]
    cos, sin = jnp.cos(ang), jnp.sin(ang)
    reps = LANES // d
    return (jnp.tile(jnp.concatenate([cos, cos], -1), (1, reps)),
            jnp.tile(jnp.concatenate([-sin, sin], -1), (1, reps)))


def _rope64(y, cos, sin, lane):
    partner = jnp.where((lane & 32) == 0, pltpu.roll(y, 96, 1), pltpu.roll(y, 32, 1))
    return y * cos + partner * sin


def _rope128(y, cos, sin):
    return y * cos + pltpu.roll(y, 64, 1) * sin


def _seg64_mean_sq(x, lane):
    x2 = x * x
    lo = jnp.sum(jnp.where(lane < 64, x2, 0.0), axis=-1, keepdims=True)
    hi = jnp.sum(jnp.where(lane >= 64, x2, 0.0), axis=-1, keepdims=True)
    return jnp.where(lane < 64, lo, hi) * (1.0 / 64)


def _da_prep_kernel(q_ref, k_ref, v_ref, cos_ref, sin_ref, qg_ref, kg_ref, qo_ref, ko_ref, vo_ref):
    lane = lax.broadcasted_iota(jnp.int32, (1, LANES), 1)
    cos, sin = cos_ref[...], sin_ref[...]
    for c in range(DA_WIDTH // LANES):
        sl = slice(c * LANES, (c + 1) * LANES)
        for x_ref, g_ref, o_ref, scale in ((q_ref, qg_ref, qo_ref, DA_DIM ** -0.5 * LOG2E), (k_ref, kg_ref, ko_ref, 1.0)):
            x = x_ref[:, sl]
            y = x * lax.rsqrt(_seg64_mean_sq(x, lane) + NORM_EPS) * g_ref[...]
            y = _rope64(y, cos, sin, lane)
            o_ref[:, sl] = (y * scale).astype(BF16)
    vo_ref[...] = v_ref[...].T.astype(BF16)


def da_prep(proj, colblk, cos, sin, qg, kg):
    t = proj.shape[0]
    tm = _tile(t, 256)
    w = DA_WIDTH
    col = lambda c: pl.BlockSpec((tm, w), lambda i, c=c: (i, c))
    row = pl.BlockSpec((tm, LANES), lambda i: (i, 0))
    vec = pl.BlockSpec((1, LANES), lambda i: (0, 0))
    out = pl.BlockSpec((tm, w), lambda i: (i, 0))
    shp = jax.ShapeDtypeStruct((t, w), BF16)
    return pl.pallas_call(
        _da_prep_kernel,
        out_shape=(shp, shp, jax.ShapeDtypeStruct((w, t), BF16)),
        grid=(t // tm,),
        in_specs=[col(colblk), col(colblk + 1), col(colblk + 2), row, row, vec, vec],
        out_specs=(out, out, pl.BlockSpec((w, tm), lambda i: (0, i))),
        compiler_params=_cparams("parallel"),
    )(proj, proj, proj, cos, sin, jnp.tile(qg, 2).reshape(1, LANES), jnp.tile(kg, 2).reshape(1, LANES))


ATTN_TILE = 1024


def _causal_steps(t, tile):
    n = t // tile
    pairs = [(i, j) for i in range(n) for j in range(i + 1)]
    return (jnp.asarray([a for a, _ in pairs], jnp.int32), jnp.asarray([b for _, b in pairs], jnp.int32))


SUM_ROWS = 16


def _with_sum_rows(vt):
    return jnp.concatenate([vt, jnp.ones((SUM_ROWS, vt.shape[1]), vt.dtype)], axis=0)


def _softmax_step_t(st, vt1, idx, m_sc, acc_sc):
    m_prev = m_sc[idx]
    m_new = jnp.maximum(m_prev, jnp.max(st, axis=0, keepdims=True))
    p = jnp.exp2(st - m_new).astype(BF16)
    acc_sc[idx] = jnp.exp2(m_prev - m_new) * acc_sc[idx] + jnp.dot(vt1, p, preferred_element_type=F32)
    m_sc[idx] = m_new


def _softmax_result_t(acc):
    return acc[:LANES] / acc[LANES:LANES + 1]


def _init_softmax_stats(m_sc, acc_sc):
    m_sc[...] = jnp.full(m_sc.shape, NEG_BIG, F32)
    acc_sc[...] = jnp.zeros(acc_sc.shape, F32)


def _da_flash_kernel(lam_init, qi_of, kj_of, q_ref, k_ref, vt_ref, lp_ref, g_ref, o_ref, m_sc, acc_sc):
    step = pl.program_id(0)
    qi, kj = qi_of[step], kj_of[step]
    tq, tk = q_ref.shape[0], k_ref.shape[0]

    @pl.when(kj == 0)
    def _():
        _init_softmax_stats(m_sc, acc_sc)

    lane = lax.broadcasted_iota(jnp.int32, (1, LANES), 1)

    def all_heads(masked):
        def one_head(h, carry):
            c = pl.multiple_of(h * LANES, LANES)
            q, k, vt1 = q_ref[:, pl.ds(c, LANES)], k_ref[:, pl.ds(c, LANES)], _with_sum_rows(vt_ref[pl.ds(c, LANES), :])
            for mp in range(2):
                qm = jnp.where((lane < 64) == (mp == 0), q, jnp.zeros_like(q))
                st = lax.dot_general(k, qm, _NT, preferred_element_type=F32)
                if masked:
                    keep = (lax.broadcasted_iota(jnp.int32, (tk, tq), 0) <= lax.broadcasted_iota(jnp.int32, (tk, tq), 1))
                    st = jnp.where(keep, st, NEG_BIG)
                _softmax_step_t(st, vt1, 2 * h + mp, m_sc, acc_sc)
            return carry

        lax.fori_loop(0, DA_HEADS, one_head, 0)

    @pl.when(kj < qi)
    def _():
        all_heads(False)

    @pl.when(kj == qi)
    def _():
        all_heads(True)
        lp = lp_ref[...]
        lam = (jnp.exp(jnp.sum(lp[0:1] * lp[1:2], keepdims=True))
               - jnp.exp(jnp.sum(lp[2:3] * lp[3:4], keepdims=True)) + lam_init)

        def finish(h, carry):
            o = _softmax_result_t(acc_sc[2 * h]) - lam * _softmax_result_t(acc_sc[2 * h + 1])
            y = o * lax.rsqrt(jnp.mean(o * o, axis=0, keepdims=True) + NORM_EPS) * g_ref[...]
            o_ref[:, pl.ds(pl.multiple_of(h * LANES, LANES), LANES)] = (y * (1.0 - lam_init)).T.astype(o_ref.dtype)
            return carry

        lax.fori_loop(0, DA_HEADS, finish, 0)


def da_flash(q, k, vt, lam_params, subln_g, lam_init):
    t = q.shape[0]
    tile = _tile(t, ATTN_TILE)
    qi_of, kj_of = _causal_steps(t, tile)
    grid_spec = pltpu.PrefetchScalarGridSpec(
        num_scalar_prefetch=2,
        grid=(qi_of.shape[0],),
        in_specs=[pl.BlockSpec((tile, DA_WIDTH), lambda s, qi, kj: (qi[s], 0)),
                  pl.BlockSpec((tile, DA_WIDTH), lambda s, qi, kj: (kj[s], 0)),
                  pl.BlockSpec((DA_WIDTH, tile), lambda s, qi, kj: (0, kj[s])),
                  pl.BlockSpec((4, DA_DIM), lambda s, qi, kj: (0, 0)),
                  pl.BlockSpec((LANES, 1), lambda s, qi, kj: (0, 0))],
        out_specs=pl.BlockSpec((tile, DA_WIDTH), lambda s, qi, kj: (qi[s], 0)),
        scratch_shapes=[pltpu.VMEM((2 * DA_HEADS, 1, tile), F32),
                        pltpu.VMEM((2 * DA_HEADS, LANES + SUM_ROWS, tile), F32)])
    return pl.pallas_call(
        functools.partial(_da_flash_kernel, lam_init),
        out_shape=jax.ShapeDtypeStruct((t, DA_WIDTH), BF16),
        grid_spec=grid_spec,
        compiler_params=_cparams("arbitrary"),
    )(qi_of, kj_of, q, k, vt, lam_params, subln_g.reshape(LANES, 1))


GDN_GROUP = 256


def _softplus(x):
    return jnp.maximum(x, 0.0) + jnp.log1p(jnp.exp(-jnp.abs(x)))


def _gdn_gates_kernel(b_ref, a_ref, at_ref, alr_ref, dtr_ref, alc_ref, dtc_ref,
                      beta_ref, gcum_ref, grev_ref, gcumt_ref, egt_ref):
    n = b_ref.shape[0]
    r = lax.broadcasted_iota(jnp.int32, (n, n), 0)
    c = lax.broadcasted_iota(jnp.int32, (n, n), 1)
    same = (r // GDN_CHUNK) == (c // GDN_CHUNK)
    incl = jnp.where(same & (c <= r), 1.0, 0.0)
    rev = jnp.where(same & (c > r), 1.0, 0.0)
    beta_ref[...] = jax.nn.sigmoid(b_ref[...])
    g = -jnp.exp(alr_ref[...]) * _softplus(a_ref[...] + dtr_ref[...])
    gcum_ref[...] = jnp.dot(incl, g, precision=HIGHEST, preferred_element_type=F32)
    grev_ref[...] = jnp.dot(rev, g, precision=HIGHEST, preferred_element_type=F32)
    gt = -jnp.exp(alc_ref[...]) * _softplus(at_ref[...] + dtc_ref[...])
    gcumt = lax.dot_general(gt, incl, _NT, precision=HIGHEST, preferred_element_type=F32)
    gcumt_ref[...] = gcumt
    egt_ref[...] = jnp.exp(gcumt)


def gdn_gates(b_in, a_in, a_t, a_log, dt_bias):
    t, h = b_in.shape
    n = _tile(t, GDN_GROUP)
    colspec = pl.BlockSpec((n, h), lambda i: (i, 0))
    rowspec = pl.BlockSpec((h, n), lambda i: (0, i))
    vr = pl.BlockSpec((1, h), lambda i: (0, 0))
    vc = pl.BlockSpec((h, 1), lambda i: (0, 0))
    cs = jax.ShapeDtypeStruct((t, h), F32)
    rs = jax.ShapeDtypeStruct((h, t), F32)
    return pl.pallas_call(
        _gdn_gates_kernel,
        out_shape=(cs, cs, cs, rs, rs),
        grid=(t // n,),
        in_specs=[colspec, colspec, rowspec, vr, vr, vc, vc],
        out_specs=(colspec, colspec, colspec, rowspec, rowspec),
        compiler_params=_cparams("parallel"),
    )(b_in, a_in, a_t, a_log.reshape(1, h), dt_bias.reshape(1, h), a_log.reshape(h, 1), dt_bias.reshape(h, 1))


def _gdn_local_kernel(xq_ref, xk_ref, xv_ref, hq_ref, hk_ref, hv_ref, wq_ref, wk_ref, wv_ref,
                      beta_ref, gcum_ref, grev_ref, gcumt_ref,
                      u_ref, w_ref, qd_ref, kdt_ref, attn_ref):
    i, hg = pl.program_id(0), pl.program_id(1)
    n = xq_ref.shape[0]
    nh = xq_ref.shape[1] // LANES

    def conv_silu(x_ref, halo_ref, cw_ref, hl):
        x, cw = x_ref[:, hl], cw_ref[:, hl]
        halo = jnp.where(i > 0, halo_ref[:, hl], 0.0)

        def taps(z):
            return (cw[3:4] * z + cw[2:3] * pltpu.roll(z, 1, 0) + cw[1:2] * pltpu.roll(z, 2, 0)
                    + cw[0:1] * pltpu.roll(z, 3, 0))

        top = taps(jnp.concatenate([halo, x[:8]], axis=0))[8:16]
        y = jnp.concatenate([top, taps(x)[8:]], axis=0)
        return y * jax.nn.sigmoid(y)

    def l2n(y):
        return y * lax.rsqrt(jnp.sum(y * y, axis=-1, keepdims=True) + NORM_EPS)

    r = lax.broadcasted_iota(jnp.int32, (n, n), 0)
    c = lax.broadcasted_iota(jnp.int32, (n, n), 1)
    same = (r // GDN_CHUNK) == (c // GDN_CHUNK)
    heads = range(nh)
    hls = [slice(hb * LANES, (hb + 1) * LANES) for hb in heads]
    pws, tinvs, rest = [], [], []
    for hb in heads:
        hl, h = hls[hb], hg * nh + hb
        q = l2n(conv_silu(xq_ref, hq_ref, wq_ref, hl)) * (GDN_DK ** -0.5)
        k = l2n(conv_silu(xk_ref, hk_ref, wk_ref, hl))
        v = conv_silu(xv_ref, hv_ref, wv_ref, hl)
        head = lax.broadcasted_iota(jnp.int32, (1, GDN_HEADS), 1) == h
        pick = lambda ref: jnp.sum(jnp.where(head, ref[...], 0.0), axis=-1, keepdims=True)
        beta, gc, grev = pick(beta_ref), pick(gcum_ref), pick(grev_ref)
        gr = gcumt_ref[pl.ds(h, 1), :]
        decay = jnp.exp(jnp.where(same & (c <= r), gc - gr, NEG_BIG))
        kb = k * beta
        k16 = k.astype(BF16)
        kk = lax.dot_general(kb.astype(BF16), k16, _NT, preferred_element_type=F32)
        qk = lax.dot_general(q.astype(BF16), k16, _NT, preferred_element_type=F32)
        pw = -jnp.where(same & (c < r), kk * decay, 0.0)
        pws.append(pw)
        tinvs.append(jnp.where(r == c, 1.0, 0.0) + pw)
        attn = qk * decay
        qd_ref[:, hl] = (q * jnp.exp(gc)).astype(BF16)
        kdt_ref[hl, :] = (k * jnp.exp(grev)).T.astype(BF16)
        left = attn[:, :LANES]
        for j in range(1, n // LANES):
            left = left + attn[:, j * LANES:(j + 1) * LANES]
        attn_ref[:, hl] = (left + pltpu.roll(left, 64, 1)).astype(BF16)
        rest.append(((v * beta).astype(BF16), (kb * jnp.exp(gc)).astype(BF16)))

    for _ in range(5):
        for hb in heads:
            p16 = pws[hb].astype(BF16)
            pws[hb] = jnp.dot(p16, p16, preferred_element_type=F32)
            tinvs[hb] = tinvs[hb] + jnp.dot(tinvs[hb].astype(BF16), pws[hb].astype(BF16), preferred_element_type=F32)
    for hb in heads:
        t16 = tinvs[hb].astype(BF16)
        u_ref[:, hls[hb]] = jnp.dot(t16, rest[hb][0], preferred_element_type=F32)
        w_ref[:, hls[hb]] = jnp.dot(t16, rest[hb][1], preferred_element_type=F32).astype(BF16)


GDN_LOCAL_HEADS = 4


def gdn_local(proj, qblk, conv_w, beta, gcum, grev, gcumt):
    t = proj.shape[0]
    n = _tile(t, GDN_GROUP)
    nh, hl = GDN_HEADS, GDN_LOCAL_HEADS
    wide = hl * LANES
    assert qblk % hl == 0 and nh % hl == 0
    x = lambda part: pl.BlockSpec((n, wide), lambda i, h, part=part: (i, (qblk + part * nh) // hl + h))
    halo = lambda part: pl.BlockSpec(
        (8, wide), lambda i, h, part=part: (jnp.maximum(i * (n // 8) - 1, 0), (qblk + part * nh) // hl + h))
    cw = lambda part: pl.BlockSpec((GDN_CONV, wide), lambda i, h, part=part: (0, part * nh // hl + h))
    col = pl.BlockSpec((n, nh), lambda i, h: (i, 0))
    row = pl.BlockSpec((nh, n), lambda i, h: (0, i))
    out = pl.BlockSpec((n, wide), lambda i, h: (i, h))
    f32o = jax.ShapeDtypeStruct((t, GDN_WIDTH), F32)
    b16o = jax.ShapeDtypeStruct((t, GDN_WIDTH), BF16)
    return pl.pallas_call(
        _gdn_local_kernel,
        out_shape=(f32o, b16o, b16o, jax.ShapeDtypeStruct((GDN_WIDTH, t), BF16), b16o),
        grid=(t // n, nh // hl),
        in_specs=[x(0), x(1), x(2), halo(0), halo(1), halo(2), cw(0), cw(1), cw(2), col, col, col, row],
        out_specs=(out, out, out, pl.BlockSpec((wide, n), lambda i, h: (h, i)), out),
        compiler_params=_cparams("parallel", "parallel"),
    )(proj, proj, proj, proj, proj, proj, conv_w, conv_w, conv_w, beta, gcum, grev, gcumt)


GDN_SCAN_HEADS = 8


def _gdn_scan_kernel(egl_ref, u_ref, w_ref, qd_ref, kdt_ref, attn_ref, z_ref, g_ref, o_ref, s_ref):
    hg, i = pl.program_id(0), pl.program_id(1)
    n = u_ref.shape[0]
    cpg = n // GDN_CHUNK
    nh = s_ref.shape[0]

    @pl.when(i == 0)
    def _():
        s_ref[...] = jnp.zeros(s_ref.shape, F32)

    states = [s_ref[hb] for hb in range(nh)]
    zeros = jnp.zeros((GDN_CHUNK, GDN_DV), BF16)
    for c in range(cpg):
        sl = slice(c * GDN_CHUNK, (c + 1) * GDN_CHUNK)
        pair = slice((c // 2) * LANES, (c // 2 + 1) * LANES)
        for hb in range(nh):
            hl = slice(hb * LANES, (hb + 1) * LANES)
            s16 = states[hb].astype(BF16)
            vnew = u_ref[sl, hl] - jnp.dot(w_ref[sl, hl], s16, preferred_element_type=F32)
            v16 = vnew.astype(BF16)
            o = (jnp.dot(qd_ref[sl, hl], s16, preferred_element_type=F32)
                 + jnp.dot(attn_ref[sl, hb * LANES:hb * LANES + GDN_CHUNK], v16, preferred_element_type=F32))
            vpad = jnp.concatenate([v16, zeros] if c % 2 == 0 else [zeros, v16], axis=0)
            states[hb] = (states[hb] * egl_ref[hg * nh + hb, i * cpg + c]
                          + jnp.dot(kdt_ref[hl, pair], vpad, preferred_element_type=F32))
            y = o * lax.rsqrt(jnp.mean(o * o, axis=-1, keepdims=True) + NORM_EPS) * g_ref[...]
            z = z_ref[sl, hl]
            o_ref[sl, hl] = (y * (z * jax.nn.sigmoid(z))).astype(o_ref.dtype)
    for hb in range(nh):
        s_ref[hb] = states[hb]


def gdn_scan(egl, u, w, qd, kdt, attn, proj, zblk, norm_g):
    t = u.shape[0]
    n = _tile(t, GDN_GROUP)
    nh = GDN_SCAN_HEADS
    wide = nh * LANES
    assert zblk % nh == 0 and GDN_HEADS % nh == 0
    blk = pl.BlockSpec((n, wide), lambda h, i: (i, h))
    return pl.pallas_call(
        _gdn_scan_kernel,
        out_shape=jax.ShapeDtypeStruct((t, GDN_WIDTH), BF16),
        grid=(GDN_HEADS // nh, t // n),
        in_specs=[pl.BlockSpec(memory_space=pltpu.SMEM), blk, blk, blk,
                  pl.BlockSpec((wide, n), lambda h, i: (h, i)), blk,
                  pl.BlockSpec((n, wide), lambda h, i: (i, zblk // nh + h)),
                  pl.BlockSpec((1, LANES), lambda h, i: (0, 0))],
        out_specs=blk,
        scratch_shapes=[pltpu.VMEM((nh, GDN_DK, GDN_DV), F32)],
        compiler_params=_cparams("parallel", "arbitrary"),
    )(egl, u, w, qd, kdt, attn, proj, norm_g.reshape(1, LANES))


def gated_deltanet(proj, small_b, small_a, qblk, zblk, conv_w, a_log, dt_bias, norm_g):
    beta, gcum, grev, gcumt, egt = gdn_gates(small_b, small_a, small_a.T, a_log, dt_bias)
    u, w, qd, kdt, attn = gdn_local(proj, qblk, conv_w, beta, gcum, grev, gcumt)
    egl = egt[:, GDN_CHUNK - 1::GDN_CHUNK]
    return gdn_scan(egl, u, w, qd, kdt, attn, proj, zblk, norm_g)


DSA_SEL_TQ = 128
DSA_SEL_TK = 512


def _dsa_prep_kernel(q_ref, k_ref, v_ref, qi_ref, ki_ref, wi_ref, cos64_ref, sin64_ref, cos128_ref, sin128_ref,
                     qg_ref, kg_ref, qo_ref, ko_ref, vo_ref, qio_ref, kio_ref, wio_ref):
    lane = lax.broadcasted_iota(jnp.int32, (1, LANES), 1)
    cos64, sin64 = cos64_ref[...], sin64_ref[...]
    cos128, sin128 = cos128_ref[...], sin128_ref[...]
    for c in range(DSA_HEADS):
        sl = slice(c * LANES, (c + 1) * LANES)
        for x_ref, g_ref, o_ref, scale in ((q_ref, qg_ref, qo_ref, DSA_DIM ** -0.5 * LOG2E), (k_ref, kg_ref, ko_ref, 1.0)):
            x = x_ref[:, sl]
            y = x * lax.rsqrt(jnp.mean(x * x, axis=-1, keepdims=True) + NORM_EPS) * g_ref[...]
            o_ref[:, sl] = (_rope128(y, cos128, sin128) * scale).astype(BF16)
    vo_ref[...] = v_ref[...].T.astype(BF16)
    tq = DSA_SEL_TQ
    for c in range(IDX_HEADS * IDX_DIM // LANES):
        y = _rope64(qi_ref[:, c * LANES:(c + 1) * LANES], cos64, sin64, lane).astype(BF16)
        for b in range(qi_ref.shape[0] // tq):
            for half in range(2):
                h = 2 * c + half
                qio_ref[b, h * tq:(h + 1) * tq, :] = y[b * tq:(b + 1) * tq, half * IDX_DIM:(half + 1) * IDX_DIM]
    kio_ref[...] = _rope64(ki_ref[...], cos64, sin64, lane).astype(BF16)
    wio_ref[...] = wi_ref[...] * ((IDX_HEADS * IDX_DIM) ** -0.5)


def dsa_prep(proj, qblk, qiblk, ki, wi, tabs64, tabs128, qg, kg):
    t = proj.shape[0]
    tm = _tile(t, 256)
    w = DSA_WIDTH
    nb = tm // DSA_SEL_TQ
    col = lambda c: pl.BlockSpec((tm, w), lambda i, c=c: (i, c))
    row = pl.BlockSpec((tm, LANES), lambda i: (i, 0))
    vec = pl.BlockSpec((1, LANES), lambda i: (0, 0))
    out = pl.BlockSpec((tm, w), lambda i: (i, 0))
    shp = jax.ShapeDtypeStruct((t, w), BF16)
    return pl.pallas_call(
        _dsa_prep_kernel,
        out_shape=(shp, shp, jax.ShapeDtypeStruct((w, t), BF16),
                   jax.ShapeDtypeStruct((t // DSA_SEL_TQ, IDX_HEADS * DSA_SEL_TQ, IDX_DIM), BF16),
                   jax.ShapeDtypeStruct((t, LANES), BF16), jax.ShapeDtypeStruct((t, IDX_HEADS), F32)),
        grid=(t // tm,),
        in_specs=[col(qblk), col(qblk + 1), col(qblk + 2),
                  pl.BlockSpec((tm, IDX_HEADS * IDX_DIM), lambda i: (i, qiblk)), row,
                  pl.BlockSpec((tm, IDX_HEADS), lambda i: (i, 0)), row, row, row, row, vec, vec],
        out_specs=(out, out, pl.BlockSpec((w, tm), lambda i: (0, i)),
                   pl.BlockSpec((nb, IDX_HEADS * DSA_SEL_TQ, IDX_DIM), lambda i: (i, 0, 0)), row,
                   pl.BlockSpec((tm, IDX_HEADS), lambda i: (i, 0))),
        compiler_params=_cparams("parallel"),
    )(proj, proj, proj, proj, ki, wi, *tabs64, *tabs128, qg.reshape(1, LANES), kg.reshape(1, LANES))


def _order_key(x):
    bits = pltpu.bitcast(x, jnp.int32)
    return jnp.where(bits < 0, bits ^ 0x7FFFFFFF, bits)


def _order_value(key):
    return pltpu.bitcast(jnp.where(key < 0, key ^ 0x7FFFFFFF, key), F32)


def _dsa_select_kernel(n_sel, qi_ref, ki_ref, wt_ref, mask_ref, keys_sc):
    i = pl.program_id(0)
    tq, tk = DSA_SEL_TQ, DSA_SEL_TK
    t = ki_ref.shape[0]
    nvalid = ((i + 1) * tq + tk - 1) // tk
    qpos = i * tq + lax.broadcasted_iota(jnp.int32, (1, tq), 1)
    w = wt_ref[...]
    q_all = qi_ref[0]
    chunk = lambda c: pl.ds(pl.multiple_of(c * tk, tk), tk)

    def score_chunk(c, carry):
        lg = lax.dot_general(ki_ref[chunk(c), :IDX_DIM], q_all, _NT, preferred_element_type=F32)
        acc = jnp.zeros((tk, tq), F32)
        for h in range(IDX_HEADS):
            acc = acc + w[h:h + 1, :] * jnp.maximum(lg[:, h * tq:(h + 1) * tq], 0.0)
        key = _order_key(acc)
        kpos = c * tk + lax.broadcasted_iota(jnp.int32, (tk, 1), 0)
        causal = kpos <= qpos
        keys_sc[chunk(c), :] = jnp.where(causal, key, INT_MIN)
        kmin8, kmax8, sum8, sq8 = carry
        fold = lambda x: x.reshape(tk // 8, 8, tq)
        live = jnp.where(causal, acc, 0.0)
        return (jnp.minimum(kmin8, jnp.min(fold(jnp.where(causal, key, INT_MAX)), axis=0)),
                jnp.maximum(kmax8, jnp.max(fold(jnp.where(causal, key, INT_MIN)), axis=0)),
                sum8 + jnp.sum(fold(live), axis=0), sq8 + jnp.sum(fold(live * live), axis=0))

    zeros8 = jnp.zeros((8, tq), F32)
    kmin8, kmax8, sum8, sq8 = lax.fori_loop(0, nvalid, score_chunk, (
        jnp.full((8, tq), INT_MAX, jnp.int32), jnp.full((8, tq), INT_MIN, jnp.int32), zeros8, zeros8))

    def count_ge(thr):
        def body(c, acc):
            hit = jnp.where(keys_sc[chunk(c), :] >= thr, 1, 0)
            return acc + jnp.sum(hit.reshape(tk // 8, 8, tq), axis=0)

        acc = lax.fori_loop(0, nvalid, body, jnp.zeros((8, tq), jnp.int32))
        return jnp.sum(acc, axis=0, keepdims=True)

    lo0 = jnp.min(kmin8, axis=0, keepdims=True)
    hi0 = jnp.max(kmax8, axis=0, keepdims=True) + 1
    cnt_lo0 = qpos + 1
    cnt_hi0 = jnp.zeros((1, tq), jnp.int32)

    def unsettled(lo, hi, cnt_lo):
        return jnp.max(jnp.where((cnt_lo > n_sel) & (lo + 1 != hi), 1, 0))

    n_live = cnt_lo0.astype(F32)
    mean = jnp.sum(sum8, axis=0, keepdims=True) / n_live
    dev = jnp.sqrt(jnp.maximum(jnp.sum(sq8, axis=0, keepdims=True) / n_live - mean * mean, 0.0))
    tail = jnp.clip(n_sel / n_live, 1e-6, 0.5)
    tt = jnp.sqrt(-2.0 * jnp.log(tail))
    z = tt - (2.30753 + 0.27061 * tt) / (1.0 + 0.99229 * tt + 0.04481 * tt * tt)
    slope = dev / (z + 1.0 / jnp.maximum(z, 0.5))
    log_n = math.log(n_sel)

    def narrow(carry):
        lo, hi, cnt_lo, cnt_hi, w_lo, w_hi, last_ok, it, _ = carry
        half = (lo >> 1) + (hi >> 1) + (lo & hi & 1)
        v_lo, v_hi = _order_value(lo), _order_value(hi)
        e_lo, e_hi = jnp.log(cnt_lo.astype(F32) + 0.5) - log_n, jnp.log(cnt_hi.astype(F32) + 0.5) - log_n
        f_lo, f_hi = e_lo * w_lo, e_hi * w_hi
        false_pos = v_lo + f_lo / (f_lo - f_hi) * (v_hi - v_lo)
        along_tail = jnp.where(last_ok > 0, v_lo + e_lo * slope, v_hi + e_hi * slope)
        guess = _order_key(jnp.where(it == 0, mean + z * dev, jnp.where(it == 1, along_tail, false_pos)))
        mid = jnp.where((guess > lo) & (guess < hi) & ((it & 7) != 7), guess, half)
        cnt = count_ge(mid)
        ok = cnt >= n_sel
        w_hi = jnp.where(ok, jnp.where(last_ok > 0, 0.5 * w_hi, w_hi), 1.0)
        w_lo = jnp.where(ok, 1.0, jnp.where(last_ok < 0, 0.5 * w_lo, w_lo))
        lo, cnt_lo = jnp.where(ok, mid, lo), jnp.where(ok, cnt, cnt_lo)
        hi, cnt_hi = jnp.where(ok, hi, mid), jnp.where(ok, cnt_hi, cnt)
        return lo, hi, cnt_lo, cnt_hi, w_lo, w_hi, jnp.where(ok, 1, -1), it + 1, unsettled(lo, hi, cnt_lo)

    ones = jnp.ones((1, tq), F32)
    lo = lax.while_loop(lambda cr: cr[8] > 0, narrow,
                        (lo0, hi0, cnt_lo0, cnt_hi0, ones, ones, jnp.zeros((1, tq), jnp.int32), jnp.int32(0),
                         unsettled(lo0, hi0, cnt_lo0)))[0]

    def write_valid(c, carry):
        mask_ref[chunk(c), :] = jnp.where(keys_sc[chunk(c), :] >= lo, 1.0, 0.0).astype(mask_ref.dtype)
        return carry

    def write_zero(c, carry):
        mask_ref[chunk(c), :] = jnp.zeros((tk, tq), mask_ref.dtype)
        return carry

    lax.fori_loop(0, nvalid, write_valid, 0)
    lax.fori_loop(nvalid, t // tk, write_zero, 0)


def dsa_select(qi, ki, wt, n_sel):
    t = ki.shape[0]
    tq = DSA_SEL_TQ
    return pl.pallas_call(
        functools.partial(_dsa_select_kernel, n_sel),
        out_shape=jax.ShapeDtypeStruct((t, t), BF16),
        grid=(t // tq,),
        in_specs=[pl.BlockSpec((1, IDX_HEADS * tq, IDX_DIM), lambda i: (i, 0, 0)),
                  pl.BlockSpec((t, LANES), lambda i: (0, 0)),
                  pl.BlockSpec((IDX_HEADS, tq), lambda i: (0, i))],
        out_specs=pl.BlockSpec((t, tq), lambda i: (0, i)),
        scratch_shapes=[pltpu.VMEM((t, tq), jnp.int32)],
        compiler_params=_cparams("parallel"),
    )(qi, ki, wt)


def _dsa_flash_kernel(qi_of, kj_of, q_ref, k_ref, vt_ref, mask_ref, o_ref, m_sc, acc_sc):
    step = pl.program_id(0)
    qi, kj = qi_of[step], kj_of[step]

    @pl.when(kj == 0)
    def _():
        _init_softmax_stats(m_sc, acc_sc)

    def one_head(h, carry):
        c = pl.ds(pl.multiple_of(h * LANES, LANES), LANES)
        st = lax.dot_general(k_ref[:, c], q_ref[:, c], _NT, preferred_element_type=F32)
        st = jnp.where(mask_ref[...] > 0, st, NEG_BIG)
        _softmax_step_t(st, _with_sum_rows(vt_ref[c, :]), h, m_sc, acc_sc)
        return carry

    lax.fori_loop(0, DSA_HEADS, one_head, 0)

    @pl.when(kj == qi)
    def _():
        def finish(h, carry):
            o_ref[:, pl.ds(pl.multiple_of(h * LANES, LANES), LANES)] = _softmax_result_t(acc_sc[h]).T.astype(o_ref.dtype)
            return carry

        lax.fori_loop(0, DSA_HEADS, finish, 0)


def dsa_flash(q, k, vt, mask_t):
    t = q.shape[0]
    tile = _tile(t, ATTN_TILE)
    qi_of, kj_of = _causal_steps(t, tile)
    grid_spec = pltpu.PrefetchScalarGridSpec(
        num_scalar_prefetch=2,
        grid=(qi_of.shape[0],),
        in_specs=[pl.BlockSpec((tile, DSA_WIDTH), lambda s, qi, kj: (qi[s], 0)),
                  pl.BlockSpec((tile, DSA_WIDTH), lambda s, qi, kj: (kj[s], 0)),
                  pl.BlockSpec((DSA_WIDTH, tile), lambda s, qi, kj: (0, kj[s])),
                  pl.BlockSpec((tile, tile), lambda s, qi, kj: (kj[s], qi[s]))],
        out_specs=pl.BlockSpec((tile, DSA_WIDTH), lambda s, qi, kj: (qi[s], 0)),
        scratch_shapes=[pltpu.VMEM((DSA_HEADS, 1, tile), F32),
                        pltpu.VMEM((DSA_HEADS, LANES + SUM_ROWS, tile), F32)])
    return pl.pallas_call(
        _dsa_flash_kernel,
        out_shape=jax.ShapeDtypeStruct((t, DSA_WIDTH), BF16),
        grid_spec=grid_spec,
        compiler_params=_cparams("arbitrary"),
    )(qi_of, kj_of, q, k, vt, mask_t)


def dsa_attention(proj, qblk, qiblk, ki, wi, tabs64, tabs128, qg, kg):
    t = proj.shape[0]
    q, k, vt, qi, kir, wis = dsa_prep(proj, qblk, qiblk, ki, wi, tabs64, tabs128, qg, kg)
    mask_t = dsa_select(qi, kir, wis.T, min(TOPK_MAX, t // 4))
    return dsa_flash(q, k, vt, mask_t)


def _take_top(s, count):
    n = s.shape[0]
    idx = lax.broadcasted_iota(jnp.int32, s.shape, 0)
    vals = []
    for _ in range(count):
        m = jnp.max(s, axis=0, keepdims=True)
        first = jnp.min(jnp.where(s == m, idx, n), axis=0, keepdims=True)
        s = jnp.where(idx == first, NEG_BIG, s)
        vals.append(m)
    return vals


def _peer_route_kernel(q_ref, sk_ref, phi_ref, e1_ref, e2_ref):
    nt = PEER_TOPK + 1
    for h in range(PEER_HEADS):
        halves = []
        for c in range(2):
            sl = slice((2 * h + c) * LANES, (2 * h + c + 1) * LANES)
            halves.append(lax.dot_general(sk_ref[h, c], q_ref[:, sl].astype(BF16), _NT, preferred_element_type=F32))
        s1, s2 = halves
        v1, v2 = _take_top(s1, nt), _take_top(s2, nt)
        v1_all, v2_all = jnp.concatenate(v1[:PEER_TOPK], axis=0), jnp.concatenate(v2[:PEER_TOPK], axis=0)
        row = lax.broadcasted_iota(jnp.int32, (8, s1.shape[1]), 0)
        ends = jnp.where(row == 0, v1[PEER_TOPK] + v2[0], jnp.where(row == 1, v1[0] + v2[PEER_TOPK], NEG_BIG))
        cand = jnp.concatenate([v1[0] + v2_all] + [v1[a] + v2_all[:8] for a in range(1, 8)]
                               + [v1_all[8:] + v2[0], ends], axis=0)
        top = _take_top(cand, nt)
        z = sum(jnp.exp(tv - top[0]) for tv in top[:PEER_TOPK])
        cut = 0.5 * (top[PEER_TOPK - 1] + top[PEER_TOPK])
        phi_ref[h] = jnp.exp(cut - s1 - v2[0])
        e1_ref[h] = jnp.exp(s1 - v1[0]) / z
        e2_ref[h] = jnp.exp(s2 - v2[0])


def peer_route(q, sub_keys):
    t = q.shape[0]
    tm = _tile(t, 256)
    big = pl.BlockSpec((PEER_HEADS, PEER_NKEYS, tm), lambda i: (0, 0, i))
    bs = jax.ShapeDtypeStruct((PEER_HEADS, PEER_NKEYS, t), F32)
    return pl.pallas_call(
        _peer_route_kernel,
        out_shape=(bs, bs, bs),
        grid=(t // tm,),
        in_specs=[pl.BlockSpec((tm, PEER_HEADS * PEER_DKEY), lambda i: (i, 0)),
                  pl.BlockSpec(sub_keys.shape, lambda i: (0, 0, 0, 0))],
        out_specs=(big, big, big),
        compiler_params=_cparams("parallel"),
    )(q, sub_keys)


def _gelu_tanh(x):
    return 0.5 * x * (1.0 + jnp.tanh(math.sqrt(2.0 / math.pi) * (x + 0.044715 * (x * x * x))))


def _peer_expert_kernel(x_ref, u_ref, v_ref, phi_ref, e1_ref, e2_ref, o_ref):
    j = pl.program_id(1)
    te = u_ref.shape[0]

    @pl.when(j == 0)
    def _():
        o_ref[...] = jnp.zeros(o_ref.shape, F32)

    act = _gelu_tanh(lax.dot_general(u_ref[...], x_ref[...], _NT, preferred_element_type=F32))
    rows_per = te // PEER_NKEYS
    gates = []
    for a in range(rows_per):
        n1 = j * rows_per + a
        g = None
        for h in range(PEER_HEADS):
            e2 = e2_ref[h]
            gh = jnp.where(e2 >= phi_ref[h, pl.ds(n1, 1), :], e1_ref[h, pl.ds(n1, 1), :] * e2, 0.0)
            g = gh if g is None else g + gh
        gates.append(g)
    ga = (jnp.concatenate(gates, axis=0) * act).astype(BF16)
    o_ref[...] += lax.dot_general(ga, v_ref[...], (((0,), (0,)), ((), ())), preferred_element_type=F32)


def peer_expert(x, u, v, phi, e1, e2, tm=512, te=512):
    t, d = x.shape
    e = u.shape[0]
    tm, te = _tile(t, tm), _tile(e, te)
    once = pl.Buffered(1)
    big = pl.BlockSpec((PEER_HEADS, PEER_NKEYS, tm), lambda i, j: (0, 0, i), pipeline_mode=once)
    return pl.pallas_call(
        _peer_expert_kernel,
        out_shape=jax.ShapeDtypeStruct((t, d), F32),
        grid=(t // tm, e // te),
        in_specs=[pl.BlockSpec((tm, d), lambda i, j: (i, 0), pipeline_mode=once),
                  pl.BlockSpec((te, d), lambda i, j: (j, 0)),
                  pl.BlockSpec((te, d), lambda i, j: (j, 0)),
                  big, big, big],
        out_specs=pl.BlockSpec((tm, d), lambda i, j: (i, 0)),
        compiler_params=_cparams("parallel", "arbitrary"),
    )(x, u, v, phi, e1, e2)


def _add_rmsnorm_kernel(h_ref, y_ref, g_ref, ho_ref, ao_ref):
    x = h_ref[...] + y_ref[...]
    ho_ref[...] = x
    ao_ref[...] = (x * lax.rsqrt(jnp.mean(x * x, axis=-1, keepdims=True) + NORM_EPS) * g_ref[...]).astype(BF16)


def add_rmsnorm(h, y, g):
    t, d = h.shape
    tm = _tile(t, 256)
    blk = pl.BlockSpec((tm, d), lambda i: (i, 0))
    return pl.pallas_call(
        _add_rmsnorm_kernel,
        out_shape=(jax.ShapeDtypeStruct((t, d), F32), jax.ShapeDtypeStruct((t, d), BF16)),
        grid=(t // tm,),
        in_specs=[blk, blk, pl.BlockSpec((1, d), lambda i: (0, 0))],
        out_specs=(blk, blk),
        compiler_params=_cparams("parallel"),
    )(h, y, g.reshape(1, d))


def _ple_kernel(a_ref, wg_ref, p_ref, wp_ref, h_ref, o_ref):
    gate = jax.nn.sigmoid(jnp.dot(a_ref[...], wg_ref[...], preferred_element_type=F32))
    o_ref[...] = h_ref[...] + gate * jnp.dot(p_ref[...], wp_ref[...], preferred_element_type=F32)


def ple(a, wg, p, wp, h, tm=512, tn=1024):
    m, k = a.shape
    n = wg.shape[1]
    kp = p.shape[1]
    tm, tn = _tile(m, tm), _tile(n, tn)
    return pl.pallas_call(
        _ple_kernel,
        out_shape=jax.ShapeDtypeStruct((m, n), F32),
        grid=(n // tn, m // tm),
        in_specs=[pl.BlockSpec((tm, k), lambda j, i: (i, 0)), pl.BlockSpec((k, tn), lambda j, i: (0, j)),
                  pl.BlockSpec((tm, kp), lambda j, i: (i, 0)), pl.BlockSpec((kp, tn), lambda j, i: (0, j)),
                  pl.BlockSpec((tm, tn), lambda j, i: (i, j))],
        out_specs=pl.BlockSpec((tm, tn), lambda j, i: (i, j)),
        compiler_params=_cparams("parallel", "parallel"),
    )(a, wg, p, wp, h)


_QBLK_GDN, _ZBLK_GDN = 0, 48
_BLK_DA = 8
_BLK_DSA = 11
_BLK_IDX = 7


def _regroup_w_in(w):
    sizes = (DA_WIDTH, DA_WIDTH, DA_WIDTH, 3 * GDN_WIDTH, GDN_WIDTH, GDN_HEADS, GDN_HEADS,
             DSA_WIDTH, DSA_WIDTH, DSA_WIDTH, IDX_HEADS * IDX_DIM, IDX_DIM, IDX_HEADS)
    offs = [0]
    for s in sizes:
        offs.append(offs[-1] + s)
    (da_q, da_k, da_v, g_qkv, g_z, g_b, g_a, c_q, c_k, c_v, c_qi, c_ki, c_w) = [
        w[:, offs[n]:offs[n + 1]] for n in range(len(sizes))]
    main = jnp.concatenate([g_qkv, g_z, da_q, da_k, da_v, c_q, c_k, c_v, c_qi], axis=1).astype(BF16)
    small = jnp.concatenate([c_ki, c_w, g_b, g_a], axis=1).astype(BF16)
    return main, small


def kernel(x, p, attn_norm, w_in, da_q_norm, da_k_norm, da_lambda, da_subln, gdn_conv, gdn_a_log, gdn_dt_bias, gdn_norm, dsa_q_norm, dsa_k_norm, w_out, ffn_norm, peer_w_q, peer_sub_keys, peer_u, peer_v, ple_norm, w_ple_gate, w_ple_proj):
    b, t, d = x.shape
    assert b == 1
    depth = w_in.shape[0]
    tabs64, tabs128 = _rope_tables(t, 64), _rope_tables(t, 128)
    h = x.reshape(t, d)
    for i in range(depth):
        w_main, w_small = _regroup_w_in(w_in[i])
        a = rmsnorm_bf16(h, attn_norm[i])
        proj = matmul(a, w_main)
        small = matmul(a, w_small)
        lam_init = 0.8 - 0.6 * math.exp(-0.3 * i)
        qa, ka, va = da_prep(proj, _BLK_DA, *tabs64, da_q_norm[i], da_k_norm[i])
        o_a = da_flash(qa, ka, va, da_lambda[i], da_subln[i], lam_init)
        o_b = gated_deltanet(proj, small[:, 96:112], small[:, 112:128], _QBLK_GDN, _ZBLK_GDN, gdn_conv[i],
                             gdn_a_log[i], gdn_dt_bias[i], gdn_norm[i])
        o_c = dsa_attention(proj, _BLK_DSA, _BLK_IDX, small, small[:, 64:96], tabs64, tabs128,
                            dsa_q_norm[i], dsa_k_norm[i])
        mix = jnp.concatenate([o_a, o_b, o_c], axis=-1)
        h = matmul(mix, w_out[i].astype(BF16), residual=h)

        a = rmsnorm_bf16(h, ffn_norm[i])
        pq = matmul(a, peer_w_q[i].astype(BF16))
        phi, e1, e2 = peer_route(pq, peer_sub_keys[i].astype(BF16))
        y = peer_expert(a, peer_u[i].astype(BF16), peer_v[i].astype(BF16), phi, e1, e2)
        h, a = add_rmsnorm(h, y, ple_norm[i])
        h = ple(a, w_ple_gate[i].astype(BF16), p[i, 0].astype(BF16), w_ple_proj[i].astype(BF16), h)
    return h.reshape(b, t, d)
```
